```python
import math
import jax
import jax.numpy as jnp
from jax import lax
import numpy as np

D_MODEL = 2048
BATCH = 16
SEQ = 256
DEPTH = 4
DEC_BATCH = 2
DEC_SEQ = 2048
PAST_LEN = 512

GRID_W = 64
HEAD_DIM = 128
N_HEADS = D_MODEL // HEAD_DIM
A_HEADS = (3 * N_HEADS) // 8
A_KV = A_HEADS // 3
B_HEADS = N_HEADS // 4
B_QK_DIM = HEAD_DIM // 2
C_HEADS = N_HEADS - A_HEADS - B_HEADS
C_KV = C_HEADS // 3
MIX_WIDTH = N_HEADS * HEAD_DIM
D_FF = 5632
WINDOW = 128
Q_BLOCK = 128
ROPE_THETA = 10000.0
EPS = 1e-6
FFN_RESID = 0.5
N_SUB = 3
SPLIT_SIZES = (A_HEADS * HEAD_DIM, A_KV * HEAD_DIM, A_KV * HEAD_DIM,
               B_HEADS * 2 * B_QK_DIM, B_HEADS * 2 * B_QK_DIM, B_HEADS * HEAD_DIM,
               C_HEADS * HEAD_DIM, C_KV * HEAD_DIM, C_KV * HEAD_DIM)
IN_WIDTH = sum(SPLIT_SIZES)

kernel_name = "hybrid_prefix_diffusion_step"


def rms_norm(x, g):
    xf = x.astype(jnp.float32)
    y = xf * lax.rsqrt(jnp.mean(xf * xf, axis=-1, keepdims=True) + EPS)
    return (y * g.astype(jnp.float32)).astype(x.dtype)


def swiglu(h, wg, wu, wd):
    return (jax.nn.silu(h @ wg) * (h @ wu)) @ wd


def axial_rope_tables(n_tokens, dim):
    rows = n_tokens // GRID_W
    row = jnp.repeat(jnp.arange(rows, dtype=jnp.float32), GRID_W)
    col = jnp.tile(jnp.arange(GRID_W, dtype=jnp.float32), rows)
    nf = dim // 4
    inv = ROPE_THETA ** (-jnp.arange(nf, dtype=jnp.float32) / nf)
    ang = jnp.concatenate([row[:, None] * inv, col[:, None] * inv], axis=-1)
    return jnp.cos(ang), jnp.sin(ang)


def apply_rope(x, cos, sin):
    x1, x2 = jnp.split(x, 2, axis=-1)
    c = cos[:, None, :].astype(x.dtype)
    s = sin[:, None, :].astype(x.dtype)
    return jnp.concatenate([x1 * c - x2 * s, x2 * c + x1 * s], axis=-1)


def attn_probs(q, k, mask=None, sink=None):
    scale = q.shape[-1] ** -0.5
    s = jnp.einsum("bqhgd,bkhd->bhgqk", q, k, preferred_element_type=jnp.float32) * scale
    if mask is not None:
        s = jnp.where(mask, s, -jnp.inf)
    if sink is None:
        return jax.nn.softmax(s, axis=-1)
    sk = sink.astype(jnp.float32)[None, :, :, None, None]
    m = jnp.maximum(jnp.max(s, axis=-1, keepdims=True), sk)
    e = jnp.exp(s - m)
    return e / (jnp.sum(e, axis=-1, keepdims=True) + jnp.exp(sk - m))


def weighted_values(p, v):
    return jnp.einsum("bhgqk,bkhd->bqhgd", p.astype(v.dtype), v)


def map_query_blocks(fn, *qs):
    b_, l_ = qs[0].shape[:2]
    nb = l_ // Q_BLOCK
    blocks = tuple(jnp.moveaxis(q.reshape((b_, nb, Q_BLOCK) + q.shape[2:]), 1, 0) for q in qs)
    out = lax.map(lambda a: fn(a[0], *a[1:]), (jnp.arange(nb),) + blocks)
    out = jnp.moveaxis(out, 0, 1)
    return out.reshape((b_, l_) + out.shape[3:])


def dense_gqa(q, k, v, sink=None):
    b_, l_, hq, d = q.shape
    kv = k.shape[2]
    qg = q.reshape(b_, l_, kv, hq // kv, d)

    def blk(i, qb):
        return weighted_values(attn_probs(qb, k, None, sink), v)

    return map_query_blocks(blk, qg).reshape(b_, l_, hq * v.shape[-1])


def window_gqa(q, k, v, k_ctx, v_ctx, sink):
    b_, l_, hq, d = q.shape
    kv = k.shape[2]
    qg = q.reshape(b_, l_, kv, hq // kv, d)
    pad = ((0, 0), (WINDOW, WINDOW), (0, 0), (0, 0))
    k_pad = jnp.pad(k, pad)
    v_pad = jnp.pad(v, pad)
    band = Q_BLOCK + 2 * WINDOW
    ctx_mask = jnp.ones((Q_BLOCK, k_ctx.shape[1]), dtype=bool)

    def blk(i, qb):
        start = i * Q_BLOCK
        k_loc = lax.dynamic_slice_in_dim(k_pad, start, band, axis=1)
        v_loc = lax.dynamic_slice_in_dim(v_pad, start, band, axis=1)
        qpos = start + jnp.arange(Q_BLOCK)
        kpos = start - WINDOW + jnp.arange(band)
        local = (jnp.abs(qpos[:, None] - kpos[None, :]) <= WINDOW) & (kpos >= 0)[None, :] & (kpos < l_)[None, :]
        mask = jnp.concatenate([ctx_mask, local], axis=1)
        kk = jnp.concatenate([k_ctx, k_loc], axis=1)
        vv = jnp.concatenate([v_ctx, v_loc], axis=1)
        return weighted_values(attn_probs(qb, kk, mask, sink), vv)

    return map_query_blocks(blk, qg).reshape(b_, l_, hq * d)


def diff_lambda(bl, lam_init):
    bl = bl.astype(jnp.float32)
    return jnp.exp(jnp.sum(bl[0] * bl[1])) - jnp.exp(jnp.sum(bl[2] * bl[3])) + lam_init


def diff_attn(q, k, v, lam, lam_init, g_sub):
    b_, l_, h, _ = q.shape
    q1 = q[..., None, :B_QK_DIM]
    q2 = q[..., None, B_QK_DIM:]
    k1 = k[..., :B_QK_DIM]
    k2 = k[..., B_QK_DIM:]

    def blk(i, q1b, q2b):
        p = attn_probs(q1b, k1) - lam * attn_probs(q2b, k2)
        return weighted_values(p, v)

    o = map_query_blocks(blk, q1, q2)[:, :, :, 0]
    o = rms_norm(o, g_sub) * (1.0 - lam_init)
    return o.reshape(b_, l_, h * HEAD_DIM)


def project(h, w_in_l, qn_g, kn_g):
    b_, l_, _ = h.shape
    p = h @ w_in_l
    parts, off = [], 0
    for size in SPLIT_SIZES:
        parts.append(p[..., off:off + size])
        off += size
    qa, ka, va, qb, kb, vb, qc, kc, vc = parts

    def heads(t, n):
        return t.reshape(b_, l_, n, -1)

    return (heads(qa, A_HEADS), heads(ka, A_KV), heads(va, A_KV),
            heads(qb, B_HEADS), heads(kb, B_HEADS), heads(vb, B_HEADS),
            rms_norm(heads(qc, C_HEADS), qn_g), rms_norm(heads(kc, C_KV), kn_g), heads(vc, C_KV))


def context_mixer(h, w_in_l, w_out_l, sink_l, lam, lam_init, sub_g, qn_g, kn_g):
    qa, ka, va, qb, kb, vb, qc, kc, vc = project(h, w_in_l, qn_g, kn_g)
    oa = dense_gqa(qa, ka, va, sink_l.reshape(A_KV, -1))
    ob = diff_attn(qb, kb, vb, lam, lam_init, sub_g)
    oc = dense_gqa(qc, kc, vc)
    o = jnp.concatenate([oa, ob, oc], axis=-1) @ w_out_l
    return o, (ka, va, kb, vb, kc, vc)


def latent_mixer(h, ctx_kv, rope_hd, rope_qk, w_in_l, w_out_l, sink_l, lam, lam_init, sub_g, qn_g, kn_g):
    b_, l_, _ = h.shape
    qa, ka, va, qb, kb, vb, qc, kc, vc = project(h, w_in_l, qn_g, kn_g)
    cos1, sin1 = rope_hd
    cos2, sin2 = rope_qk
    qa, ka, qc, kc = [apply_rope(t, cos1, sin1) for t in (qa, ka, qc, kc)]
    qb, kb = [apply_rope(t.reshape(b_, l_, 2 * B_HEADS, B_QK_DIM), cos2, sin2).reshape(b_, l_, B_HEADS, 2 * B_QK_DIM)
              for t in (qb, kb)]
    cka, cva, ckb, cvb, ckc, cvc = ctx_kv
    oa = window_gqa(qa, ka, va, cka, cva, sink_l.reshape(A_KV, -1))
    ob = diff_attn(qb, jnp.concatenate([ckb, kb], axis=1), jnp.concatenate([cvb, vb], axis=1), lam, lam_init, sub_g)
    oc = dense_gqa(qc, jnp.concatenate([ckc, kc], axis=1), jnp.concatenate([cvc, vc], axis=1))
    o = jnp.concatenate([oa, ob, oc], axis=-1) @ w_out_l
    return o, None


def adaln(cond, w, b):
    m = jax.nn.silu(cond) @ w + b
    return m.reshape(cond.shape[0], 1, 3 * N_SUB, D_MODEL)


def residual_sublayer(x, m, j, g_pre, g_post, fn, weight):
    h = rms_norm(x, g_pre) * (1 + m[:, :, 3 * j + 1]) + m[:, :, 3 * j]
    y, aux = fn(h)
    return x + weight * m[:, :, 3 * j + 2] * rms_norm(y, g_post), aux


def trunk_layer(x, cond, mixer, w_mod_l, b_mod_l, pre_l, post_l, gate_l, up_l, down_l):
    m = adaln(cond, w_mod_l, b_mod_l)

    def ffn1(h):
        return swiglu(h, gate_l[0], up_l[0], down_l[0]), None

    def ffn2(h):
        return swiglu(h, gate_l[1], up_l[1], down_l[1]), None

    x, _ = residual_sublayer(x, m, 0, pre_l[0], post_l[0], ffn1, FFN_RESID)
    x, aux = residual_sublayer(x, m, 1, pre_l[1], post_l[1], mixer, 1.0)
    x, _ = residual_sublayer(x, m, 2, pre_l[2], post_l[2], ffn2, FFN_RESID)
    return x, aux


def setup_inputs(seed: int = 0) -> dict:
    key = jax.random.key(seed)
    ks = jax.random.split(key, 24)
    f32 = jnp.float32

    def nrm(k, shape, s=1.0):
        return s * jax.random.normal(k, shape, f32)

    def gain(k, shape):
        return 1.0 + 0.05 * jax.random.normal(k, shape, f32)

    cshape = (DEC_BATCH, DEPTH, PAST_LEN)
    return {
        "x_prompt": nrm(ks[0], (BATCH, SEQ, D_MODEL)),
        "x_sample": nrm(ks[1], (DEC_BATCH, DEC_SEQ, D_MODEL)),
        "cache_a_k": nrm(ks[2], cshape + (A_KV, HEAD_DIM)),
        "cache_a_v": nrm(ks[3], cshape + (A_KV, HEAD_DIM)),
        "cache_b_k": nrm(ks[4], cshape + (B_HEADS, 2 * B_QK_DIM)),
        "cache_b_v": nrm(ks[5], cshape + (B_HEADS, HEAD_DIM)),
        "cache_c_k": nrm(ks[6], cshape + (C_KV, HEAD_DIM)),
        "cache_c_v": nrm(ks[7], cshape + (C_KV, HEAD_DIM)),
        "c": nrm(ks[8], (DEC_BATCH, D_MODEL)),
        "c_ctx": nrm(ks[9], (D_MODEL,)),
        "w_mod": nrm(ks[10], (DEPTH, D_MODEL, 3 * N_SUB * D_MODEL), 0.5 * D_MODEL ** -0.5),
        "b_mod": nrm(ks[11], (DEPTH, 3 * N_SUB * D_MODEL), 0.01),
        "norm_pre": gain(ks[12], (DEPTH, N_SUB, D_MODEL)),
        "norm_post": gain(ks[13], (DEPTH, N_SUB, D_MODEL)),
        "ffn_gate": nrm(ks[14], (DEPTH, 2, D_MODEL, D_FF), D_MODEL ** -0.5),
        "ffn_up": nrm(ks[15], (DEPTH, 2, D_MODEL, D_FF), D_MODEL ** -0.5),
        "ffn_down": nrm(ks[16], (DEPTH, 2, D_FF, D_MODEL), D_FF ** -0.5),
        "w_in": nrm(ks[17], (DEPTH, D_MODEL, IN_WIDTH), D_MODEL ** -0.5),
        "w_out": nrm(ks[18], (DEPTH, MIX_WIDTH, D_MODEL), MIX_WIDTH ** -0.5),
        "a_sink": nrm(ks[19], (DEPTH, A_HEADS)),
        "b_lambda": nrm(ks[20], (DEPTH, 4, B_QK_DIM), 0.1),
        "b_subln": gain(ks[21], (DEPTH, HEAD_DIM)),
        "c_qnorm": gain(ks[22], (DEPTH, HEAD_DIM)),
        "c_knorm": gain(ks[23], (DEPTH, HEAD_DIM)),
    }


def reference(x_prompt, x_sample, cache_a_k, cache_a_v, cache_b_k, cache_b_v, cache_c_k, cache_c_v,
              c, c_ctx, w_mod, b_mod, norm_pre, norm_post, ffn_gate, ffn_up, ffn_down,
              w_in, w_out, a_sink, b_lambda, b_subln, c_qnorm, c_knorm):
    ctx_cond = c_ctx[None, :]
    xp = x_prompt
    ctx_lists = ([], [], [], [], [], [])
    for l in range(DEPTH):
        lam_init = 0.8 - 0.6 * math.exp(-0.3 * l)
        lam = diff_lambda(b_lambda[l], lam_init)

        def mix_ctx(h, l=l, lam=lam, lam_init=lam_init):
            return context_mixer(h, w_in[l], w_out[l], a_sink[l], lam, lam_init,
                                 b_subln[l], c_qnorm[l], c_knorm[l])

        xp, kv = trunk_layer(xp, ctx_cond, mix_ctx, w_mod[l], b_mod[l], norm_pre[l], norm_post[l],
                             ffn_gate[l], ffn_up[l], ffn_down[l])
        for lst, t in zip(ctx_lists, kv):
            lst.append(t)
    y_prompt = xp
    new_a_k = jnp.stack(ctx_lists[0], axis=1)
    new_a_v = jnp.stack(ctx_lists[1], axis=1)
    new_b_k = jnp.stack(ctx_lists[2], axis=1)
    new_b_v = jnp.stack(ctx_lists[3], axis=1)
    new_c_k = jnp.stack(ctx_lists[4], axis=1)
    new_c_v = jnp.stack(ctx_lists[5], axis=1)

    n_lat = x_sample.shape[1]
    rope_hd = axial_rope_tables(n_lat, HEAD_DIM)
    rope_qk = axial_rope_tables(n_lat, B_QK_DIM)
    xs = x_sample
    for l in range(DEPTH):
        lam_init = 0.8 - 0.6 * math.exp(-0.3 * l)
        lam = diff_lambda(b_lambda[l], lam_init)
        ctx_kv = (cache_a_k[:, l], cache_a_v[:, l], cache_b_k[:, l], cache_b_v[:, l],
                  cache_c_k[:, l], cache_c_v[:, l])

        def mix_lat(h, l=l, lam=lam, lam_init=lam_init, ctx_kv=ctx_kv):
            return latent_mixer(h, ctx_kv, rope_hd, rope_qk, w_in[l], w_out[l], a_sink[l], lam, lam_init,
                                b_subln[l], c_qnorm[l], c_knorm[l])

        xs, _ = trunk_layer(xs, c, mix_lat, w_mod[l], b_mod[l], norm_pre[l], norm_post[l],
                            ffn_gate[l], ffn_up[l], ffn_down[l])
    y_sample = xs
    return (y_prompt, y_sample, new_a_k, new_a_v, new_b_k, new_b_v, new_c_k, new_c_v)
```

```python
import functools
import math

import jax
import jax.numpy as jnp
from jax import lax
from jax.experimental import pallas as pl
from jax.experimental.pallas import tpu as pltpu

D_MODEL = 2048
DEPTH = 4
GRID_W = 64
HEAD_DIM = 128
A_HEADS, A_KV = 6, 2
B_HEADS, B_QK_DIM = 4, 64
C_HEADS, C_KV = 6, 2
D_FF = 5632
WINDOW = 128
ROPE_THETA = 10000.0
EPS = 1e-6
FFN_RESID = 0.5
N_SUB = 3
N_COND = 8
Q_WIDTH = (A_HEADS + B_HEADS + C_HEADS) * HEAD_DIM
KV_WIDTH = (A_KV + B_HEADS + C_KV) * HEAD_DIM
IN_WIDTH = Q_WIDTH + 2 * KV_WIDTH
_OFF = {}
_o = 0
for _name, _w in (("qa", A_HEADS), ("ka", A_KV), ("va", A_KV), ("qb", B_HEADS), ("kb", B_HEADS),
                  ("vb", B_HEADS), ("qc", C_HEADS), ("kc", C_KV), ("vc", C_KV)):
    _OFF[_name] = _o
    _o += _w * HEAD_DIM

V7X_VMEM_LIMIT_BYTES = 56 * 1024 * 1024
LANES = 128

TM = 1024
TF = 256
TN_IN = 512
TK_OUT = 512
TN_MOD = 1024
ROW_CHUNK = 256
TR_PREP = 256
TQ_LAT = 256

F32 = jnp.float32
BF16 = jnp.bfloat16


def _rms(x):
    return x * lax.rsqrt(jnp.mean(x * x, axis=-1, keepdims=True) + EPS)


def _dot(a, b):
    return jnp.dot(a, b, preferred_element_type=F32)


def _dot_t(a, b):
    return lax.dot_general(a, b, (((1,), (1,)), ((), ())), preferred_element_type=F32)


def _cond_of_tile(i, tm, n_ctx_rows, lat_rows):
    return jnp.maximum((i * tm - n_ctx_rows) // lat_rows + 1, 0)


def _mod_kernel(cond_ref, w_ref, b_ref, o_ref):
    c = cond_ref[...]
    s = (c * jax.nn.sigmoid(c)).astype(BF16)
    o_ref[...] = _dot(s, w_ref[...].astype(BF16)) + b_ref[...]


def _modulation(cond, w_mod, b_mod):
    depth, d, n = w_mod.shape
    return pl.pallas_call(
        _mod_kernel,
        out_shape=jax.ShapeDtypeStruct((depth, N_COND, n), F32),
        grid=(depth, n // TN_MOD),
        in_specs=[
            pl.BlockSpec((N_COND, d), lambda l, j: (0, 0)),
            pl.BlockSpec((None, d, TN_MOD), lambda l, j: (l, 0, j)),
            pl.BlockSpec((None, 1, TN_MOD), lambda l, j: (l, 0, j)),
        ],
        out_specs=pl.BlockSpec((None, N_COND, TN_MOD), lambda l, j: (l, 0, j)),
        compiler_params=pltpu.CompilerParams(
            dimension_semantics=("arbitrary", "arbitrary"),
            vmem_limit_bytes=40 * 1024 * 1024),
        name="adaln_modulation",
    )(cond, w_mod, b_mod.reshape(depth, 1, n))


def _modulated_norm_rows(x_ref, mod_ref, gpre_ref, h_ref, j):
    shift = mod_ref[3 * j:3 * j + 1, :]
    scale1 = 1.0 + mod_ref[3 * j + 1:3 * j + 2, :]
    g = gpre_ref[j:j + 1, :]

    def body(r, carry):
        rows = pl.ds(pl.multiple_of(r * ROW_CHUNK, ROW_CHUNK), ROW_CHUNK)
        x = x_ref[rows, :]
        h_ref[rows, :] = (_rms(x) * g * scale1 + shift).astype(BF16)
        return carry

    lax.fori_loop(0, x_ref.shape[0] // ROW_CHUNK, body, 0)


def _gated_residual_rows(x_ref, y_ref, o_ref, mod_ref, gpost_ref, j, weight):
    gate = weight * mod_ref[3 * j + 2:3 * j + 3, :]
    g = gpost_ref[j:j + 1, :]

    def body(r, carry):
        rows = pl.ds(pl.multiple_of(r * ROW_CHUNK, ROW_CHUNK), ROW_CHUNK)
        y = y_ref[rows, :]
        o_ref[rows, :] = x_ref[rows, :] + gate * (_rms(y) * g)
        return carry

    lax.fori_loop(0, x_ref.shape[0] // ROW_CHUNK, body, 0)


def _ffn_kernel(x_ref, mod_ref, gpre_ref, gpost_ref, wg_ref, wu_ref, wd_ref, o_ref, h_ref, *, j):
    f = pl.program_id(1)

    @pl.when(f == 0)
    def _():
        _modulated_norm_rows(x_ref, mod_ref, gpre_ref, h_ref, j)
        o_ref[...] = jnp.zeros_like(o_ref)

    h = h_ref[...]
    g = _dot(h, wg_ref[...].astype(BF16))
    u = _dot(h, wu_ref[...].astype(BF16))
    a = (g * jax.nn.sigmoid(g) * u).astype(BF16)
    wd = wd_ref[...].astype(BF16)
    nchunk = 512
    for n in range(0, D_MODEL, nchunk):
        o_ref[:, n:n + nchunk] += _dot(a, wd[:, n:n + nchunk])

    @pl.when(f == pl.num_programs(1) - 1)
    def _():
        _gated_residual_rows(x_ref, o_ref, o_ref, mod_ref, gpost_ref, j, FFN_RESID)


def _ffn(x, mod, norm_pre, norm_post, w_gate, w_up, w_down, *, layer, which, j, n_ctx_rows, lat_rows):
    m, d = x.shape
    cond = functools.partial(_cond_of_tile, tm=TM, n_ctx_rows=n_ctx_rows, lat_rows=lat_rows)
    return pl.pallas_call(
        functools.partial(_ffn_kernel, j=j),
        out_shape=jax.ShapeDtypeStruct((m, d), F32),
        grid=(m // TM, D_FF // TF),
        in_specs=[
            pl.BlockSpec((TM, d), lambda i, f: (i, 0)),
            pl.BlockSpec((None, None, 3 * N_SUB, d), lambda i, f: (layer, cond(i), 0, 0)),
            pl.BlockSpec((None, N_SUB, d), lambda i, f: (layer, 0, 0)),
            pl.BlockSpec((None, N_SUB, d), lambda i, f: (layer, 0, 0)),
            pl.BlockSpec((None, None, d, TF), lambda i, f: (layer, which, 0, f)),
            pl.BlockSpec((None, None, d, TF), lambda i, f: (layer, which, 0, f)),
            pl.BlockSpec((None, None, TF, d), lambda i, f: (layer, which, f, 0)),
        ],
        out_specs=pl.BlockSpec((TM, d), lambda i, f: (i, 0)),
        scratch_shapes=[pltpu.VMEM((TM, d), BF16)],
        compiler_params=pltpu.CompilerParams(
            dimension_semantics=("arbitrary", "arbitrary"),
            vmem_limit_bytes=V7X_VMEM_LIMIT_BYTES),
        name="ffn_sublayer",
    )(x, mod, norm_pre, norm_post, w_gate, w_up, w_down)


def _in_proj_kernel(x_ref, mod_ref, gpre_ref, w_ref, o_ref, h_ref, *, j):
    @pl.when(pl.program_id(1) == 0)
    def _():
        _modulated_norm_rows(x_ref, mod_ref, gpre_ref, h_ref, j)

    o_ref[...] = _dot(h_ref[...], w_ref[...].astype(BF16))


def _in_proj(x, mod, norm_pre, w_in, *, layer, j, n_ctx_rows, lat_rows):
    m, d = x.shape
    n = w_in.shape[-1]
    cond = functools.partial(_cond_of_tile, tm=TM, n_ctx_rows=n_ctx_rows, lat_rows=lat_rows)
    return pl.pallas_call(
        functools.partial(_in_proj_kernel, j=j),
        out_shape=jax.ShapeDtypeStruct((m, n), F32),
        grid=(m // TM, n // TN_IN),
        in_specs=[
            pl.BlockSpec((TM, d), lambda i, c: (i, 0)),
            pl.BlockSpec((None, None, 3 * N_SUB, d), lambda i, c: (layer, cond(i), 0, 0)),
            pl.BlockSpec((None, N_SUB, d), lambda i, c: (layer, 0, 0)),
            pl.BlockSpec((None, d, TN_IN), lambda i, c: (layer, 0, c)),
        ],
        out_specs=pl.BlockSpec((TM, TN_IN), lambda i, c: (i, c)),
        scratch_shapes=[pltpu.VMEM((TM, d), BF16)],
        compiler_params=pltpu.CompilerParams(
            dimension_semantics=("arbitrary", "arbitrary"),
            vmem_limit_bytes=48 * 1024 * 1024),
        name="qkv_projection",
    )(x, mod, norm_pre, w_in)


def _rope128(x, cos, sin):
    return x * cos + pltpu.roll(x, HEAD_DIM // 2, axis=1) * sin


def _rope64(x, cos, sin, low_half):
    swapped = jnp.where(low_half, pltpu.roll(x, LANES - B_QK_DIM // 2, axis=1),
                        pltpu.roll(x, B_QK_DIM // 2, axis=1))
    return x * cos + swapped * sin


def _prep_kernel(p_ref, ca_ref, sa_ref, cb_ref, sb_ref, qn_ref, kn_ref,
                 q_ref, k_ref, v_ref, kcn_ref, *, layer):
    ca, sa = ca_ref[...], sa_ref[...]
    cb, sb = cb_ref[...], sb_ref[...]
    qn = qn_ref[layer:layer + 1, :]
    kn = kn_ref[layer:layer + 1, :]
    lane = lax.broadcasted_iota(jnp.int32, (1, LANES), 1)
    low_half = (lane % B_QK_DIM) < (B_QK_DIM // 2)
    scale_hd = HEAD_DIM ** -0.5
    scale_qk = B_QK_DIM ** -0.5

    def head(name, i):
        c0 = _OFF[name] + i * HEAD_DIM
        return p_ref[:, c0:c0 + HEAD_DIM]

    def put(ref, col_head, val):
        ref[:, col_head * HEAD_DIM:(col_head + 1) * HEAD_DIM] = val.astype(ref.dtype)

    for i in range(A_HEADS):
        put(q_ref, i, _rope128(head("qa", i), ca, sa) * scale_hd)
    for i in range(B_HEADS):
        put(q_ref, A_HEADS + i, _rope64(head("qb", i), cb, sb, low_half) * scale_qk)
    for i in range(C_HEADS):
        put(q_ref, A_HEADS + B_HEADS + i, _rope128(_rms(head("qc", i)) * qn, ca, sa) * scale_hd)
    for i in range(A_KV):
        put(k_ref, i, _rope128(head("ka", i), ca, sa))
        put(v_ref, i, head("va", i))
    for i in range(B_HEADS):
        put(k_ref, A_KV + i, _rope64(head("kb", i), cb, sb, low_half))
        put(v_ref, A_KV + i, head("vb", i))
    for i in range(C_KV):
        kc = _rms(head("kc", i)) * kn
        put(kcn_ref, i, kc)
        put(k_ref, A_KV + B_HEADS + i, _rope128(kc, ca, sa))
        put(v_ref, A_KV + B_HEADS + i, head("vc", i))


def _prep(p, tables, c_qnorm, c_knorm, *, layer):
    m = p.shape[0]
    row = lambda w: pl.BlockSpec((TR_PREP, w), lambda i: (i, 0))
    full = lambda a: pl.BlockSpec(a.shape, lambda i: (0,) * a.ndim)
    return pl.pallas_call(
        functools.partial(_prep_kernel, layer=layer),
        out_shape=(jax.ShapeDtypeStruct((m, Q_WIDTH), BF16),
                   jax.ShapeDtypeStruct((m, KV_WIDTH), BF16),
                   jax.ShapeDtypeStruct((m, KV_WIDTH), BF16),
                   jax.ShapeDtypeStruct((m, C_KV * HEAD_DIM), F32)),
        grid=(m // TR_PREP,),
        in_specs=[row(IN_WIDTH), row(LANES), row(LANES), row(LANES), row(LANES),
                  full(c_qnorm), full(c_knorm)],
        out_specs=(row(Q_WIDTH), row(KV_WIDTH), row(KV_WIDTH), row(C_KV * HEAD_DIM)),
        compiler_params=pltpu.CompilerParams(
            dimension_semantics=("arbitrary",),
            vmem_limit_bytes=40 * 1024 * 1024),
        name="qkv_prepare",
    )(p, *tables, c_qnorm, c_knorm)


def _softmax_parts(scores, sink=None):
    m = functools.reduce(jnp.maximum, [jnp.max(s, axis=-1, keepdims=True) for s in scores])
    if sink is not None:
        m = jnp.maximum(m, sink)
    es = [jnp.exp(s - m) for s in scores]
    denom = functools.reduce(lambda a, b: a + b, [jnp.sum(e, axis=-1, keepdims=True) for e in es])
    if sink is not None:
        denom = denom + jnp.exp(sink - m)
    return es, 1.0 / denom


def _weighted(ps, vs):
    return functools.reduce(lambda a, b: a + b, [_dot(p.astype(BF16), v) for p, v in zip(ps, vs)])


def _diff_lambda(bl_ref, layer, lam_init):
    bl = bl_ref[layer]
    s1 = jnp.sum(bl[0:1, :] * bl[1:2, :], axis=-1, keepdims=True)
    s2 = jnp.sum(bl[2:3, :] * bl[3:4, :], axis=-1, keepdims=True)
    return jnp.exp(s1) - jnp.exp(s2) + lam_init


def _head_loader(ref, rows, head0):
    return lambda h: ref[rows, (head0 + h) * HEAD_DIM:(head0 + h + 1) * HEAD_DIM]


_KV_HEAD0 = {"a": 0, "b": A_KV, "c": A_KV + B_HEADS}


def _attend(q_ref, segs, o_ref, sink_ref, bl_ref, subln_ref, *, layer, local_mask=None):
    lam_init = 0.8 - 0.6 * math.exp(-0.3 * layer)
    lam = _diff_lambda(bl_ref, layer, lam_init)
    g_sub = subln_ref[layer:layer + 1, :]
    lane = lax.broadcasted_iota(jnp.int32, (1, HEAD_DIM), 1)
    first_map = lane < B_QK_DIM

    for i in range(A_HEADS):
        kv = i // (A_HEADS // A_KV)
        q = q_ref[:, i * HEAD_DIM:(i + 1) * HEAD_DIM]
        scores = [_dot_t(q, k(kv)) for k, _ in segs["a"]]
        if local_mask is not None:
            scores[-1] = jnp.where(local_mask, scores[-1], -jnp.inf)
        es, rden = _softmax_parts(scores, sink_ref[layer, i])
        o = _weighted(es, [v(kv) for _, v in segs["a"]]) * rden
        o_ref[:, i * HEAD_DIM:(i + 1) * HEAD_DIM] = o.astype(o_ref.dtype)

    for i in range(B_HEADS):
        c0 = (A_HEADS + i) * HEAD_DIM
        q = q_ref[:, c0:c0 + HEAD_DIM]
        q1 = jnp.where(first_map, q, jnp.zeros_like(q))
        q2 = jnp.where(first_map, jnp.zeros_like(q), q)
        e1, r1 = _softmax_parts([_dot_t(q1, k(i)) for k, _ in segs["b"]])
        e2, r2 = _softmax_parts([_dot_t(q2, k(i)) for k, _ in segs["b"]])
        r2 = lam * r2
        ps = [a * r1 - b * r2 for a, b in zip(e1, e2)]
        o = _weighted(ps, [v(i) for _, v in segs["b"]])
        o = _rms(o) * g_sub * (1.0 - lam_init)
        o_ref[:, c0:c0 + HEAD_DIM] = o.astype(o_ref.dtype)

    for i in range(C_HEADS):
        kv = i // (C_HEADS // C_KV)
        c0 = (A_HEADS + B_HEADS + i) * HEAD_DIM
        q = q_ref[:, c0:c0 + HEAD_DIM]
        es, rden = _softmax_parts([_dot_t(q, k(kv)) for k, _ in segs["c"]])
        o = _weighted(es, [v(kv) for _, v in segs["c"]]) * rden
        o_ref[:, c0:c0 + HEAD_DIM] = o.astype(o_ref.dtype)


def _ctx_attn_kernel(q_ref, k_ref, v_ref, sink_ref, bl_ref, subln_ref, o_ref, *, layer):
    rows = slice(None)
    segs = {g: [(_head_loader(k_ref, rows, h0), _head_loader(v_ref, rows, h0))]
            for g, h0 in _KV_HEAD0.items()}
    _attend(q_ref, segs, o_ref, sink_ref, bl_ref, subln_ref, layer=layer)


def _ctx_attention(q, k, v, a_sink, b_lambda, b_subln, *, layer, batch, seq):
    smem = pl.BlockSpec(memory_space=pltpu.SMEM)
    full = lambda a: pl.BlockSpec(a.shape, lambda b: (0,) * a.ndim)
    return pl.pallas_call(
        functools.partial(_ctx_attn_kernel, layer=layer),
        out_shape=jax.ShapeDtypeStruct((batch * seq, Q_WIDTH), BF16),
        grid=(batch,),
        in_specs=[pl.BlockSpec((seq, Q_WIDTH), lambda b: (b, 0)),
                  pl.BlockSpec((seq, KV_WIDTH), lambda b: (b, 0)),
                  pl.BlockSpec((seq, KV_WIDTH), lambda b: (b, 0)),
                  smem, full(b_lambda), full(b_subln)],
        out_specs=pl.BlockSpec((seq, Q_WIDTH), lambda b: (b, 0)),
        compiler_params=pltpu.CompilerParams(
            dimension_semantics=("arbitrary",),
            vmem_limit_bytes=40 * 1024 * 1024),
        name="context_attention",
    )(q, k, v, a_sink, b_lambda, b_subln)


def _lat_attn_kernel(q_ref, k_ref, v_ref, cak_ref, cav_ref, cbk_ref, cbv_ref, cck_ref, ccv_ref,
                     sink_ref, bl_ref, subln_ref, o_ref, ck_ref, cv_ref, *, layer, lat_seq):
    @pl.when(pl.program_id(1) == 0)
    def _():
        c0 = 0
        for kref, vref in ((cak_ref, cav_ref), (cbk_ref, cbv_ref), (cck_ref, ccv_ref)):
            w = kref.shape[-1]
            ck_ref[:, c0:c0 + w] = kref[...].astype(BF16)
            cv_ref[:, c0:c0 + w] = vref[...].astype(BF16)
            c0 += w

    tq = q_ref.shape[0]
    band = tq + 2 * WINDOW
    q0 = pl.program_id(1) * tq
    k0 = pl.multiple_of(jnp.clip(q0 - WINDOW, 0, lat_seq - band), WINDOW)
    qpos = q0 + lax.broadcasted_iota(jnp.int32, (tq, 1), 0)
    kpos = k0 + lax.broadcasted_iota(jnp.int32, (1, band), 1)
    local_mask = jnp.abs(qpos - kpos) <= WINDOW

    every = slice(None)
    lat_rows = {"a": pl.ds(k0, band), "b": every, "c": every}
    segs = {g: [(_head_loader(ck_ref, every, h0), _head_loader(cv_ref, every, h0)),
                (_head_loader(k_ref, lat_rows[g], h0), _head_loader(v_ref, lat_rows[g], h0))]
            for g, h0 in _KV_HEAD0.items()}
    _attend(q_ref, segs, o_ref, sink_ref, bl_ref, subln_ref, layer=layer, local_mask=local_mask)


def _lat_attention(q, k, v, caches, a_sink, b_lambda, b_subln, *, layer, n_ctx_rows, lat_batch, lat_seq):
    smem = pl.BlockSpec(memory_space=pltpu.SMEM)
    full = lambda a: pl.BlockSpec(a.shape, lambda b, i: (0,) * a.ndim)
    q_blk0 = n_ctx_rows // TQ_LAT
    kv_blk0 = n_ctx_rows // lat_seq
    n_q = lat_seq // TQ_LAT
    past = caches[0].shape[2]

    def cache_spec(a):
        return pl.BlockSpec((None, None) + a.shape[2:], lambda b, i: (b, layer, 0, 0))

    return pl.pallas_call(
        functools.partial(_lat_attn_kernel, layer=layer, lat_seq=lat_seq),
        out_shape=jax.ShapeDtypeStruct((lat_batch * lat_seq, Q_WIDTH), BF16),
        grid=(lat_batch, n_q),
        in_specs=[pl.BlockSpec((TQ_LAT, Q_WIDTH), lambda b, i: (q_blk0 + b * n_q + i, 0)),
                  pl.BlockSpec((lat_seq, KV_WIDTH), lambda b, i: (kv_blk0 + b, 0)),
                  pl.BlockSpec((lat_seq, KV_WIDTH), lambda b, i: (kv_blk0 + b, 0))]
                 + [cache_spec(a) for a in caches]
                 + [smem, full(b_lambda), full(b_subln)],
        out_specs=pl.BlockSpec((TQ_LAT, Q_WIDTH), lambda b, i: (b * n_q + i, 0)),
        scratch_shapes=[pltpu.VMEM((past, KV_WIDTH), BF16), pltpu.VMEM((past, KV_WIDTH), BF16)],
        compiler_params=pltpu.CompilerParams(
            dimension_semantics=("arbitrary", "arbitrary"),
            vmem_limit_bytes=V7X_VMEM_LIMIT_BYTES),
        name="latent_attention",
    )(q, k, v, *caches, a_sink, b_lambda, b_subln)


def _out_proj_kernel(a_ref, x_ref, mod_ref, gpost_ref, w_ref, o_ref, *, j):
    kk = pl.program_id(1)

    @pl.when(kk == 0)
    def _():
        o_ref[...] = jnp.zeros_like(o_ref)

    o_ref[...] += _dot(a_ref[...], w_ref[...].astype(BF16))

    @pl.when(kk == pl.num_programs(1) - 1)
    def _():
        _gated_residual_rows(x_ref, o_ref, o_ref, mod_ref, gpost_ref, j, 1.0)


def _out_proj(a, x, mod, norm_post, w_out, *, layer, j, n_ctx_rows, lat_rows):
    m, d = x.shape
    kdim = a.shape[1]
    cond = functools.partial(_cond_of_tile, tm=TM, n_ctx_rows=n_ctx_rows, lat_rows=lat_rows)
    return pl.pallas_call(
        functools.partial(_out_proj_kernel, j=j),
        out_shape=jax.ShapeDtypeStruct((m, d), F32),
        grid=(m // TM, kdim // TK_OUT),
        in_specs=[
            pl.BlockSpec((TM, TK_OUT), lambda i, k: (i, k)),
            pl.BlockSpec((TM, d), lambda i, k: (i, 0)),
            pl.BlockSpec((None, None, 3 * N_SUB, d), lambda i, k: (layer, cond(i), 0, 0)),
            pl.BlockSpec((None, N_SUB, d), lambda i, k: (layer, 0, 0)),
            pl.BlockSpec((None, TK_OUT, d), lambda i, k: (layer, k, 0)),
        ],
        out_specs=pl.BlockSpec((TM, d), lambda i, k: (i, 0)),
        compiler_params=pltpu.CompilerParams(
            dimension_semantics=("arbitrary", "arbitrary"),
            vmem_limit_bytes=V7X_VMEM_LIMIT_BYTES),
        name="out_projection",
    )(a, x, mod, norm_post, w_out)


def _rope_tables(n_ctx_rows, lat_batch, lat_seq):
    t = jnp.arange(lat_seq)
    row = (t // GRID_W).astype(F32)
    col = (t % GRID_W).astype(F32)

    def tables(dim):
        nf = dim // 4
        inv = ROPE_THETA ** (-jnp.arange(nf, dtype=F32) / nf)
        ang = jnp.concatenate([row[:, None] * inv, col[:, None] * inv], axis=-1)
        cos, sin = jnp.cos(ang), jnp.sin(ang)
        cos_t = jnp.tile(jnp.concatenate([cos, cos], axis=-1), (1, LANES // dim))
        sin_t = jnp.tile(jnp.concatenate([-sin, sin], axis=-1), (1, LANES // dim))
        ident = (jnp.ones((n_ctx_rows, LANES), F32), jnp.zeros((n_ctx_rows, LANES), F32))
        return (jnp.concatenate([ident[0]] + [cos_t] * lat_batch, axis=0),
                jnp.concatenate([ident[1]] + [sin_t] * lat_batch, axis=0))

    ca, sa = tables(HEAD_DIM)
    cb, sb = tables(B_QK_DIM)
    return ca, sa, cb, sb


def kernel(x_prompt, x_sample, cache_a_k, cache_a_v, cache_b_k, cache_b_v, cache_c_k, cache_c_v,
           c, c_ctx, w_mod, b_mod, norm_pre, norm_post, ffn_gate, ffn_up, ffn_down,
           w_in, w_out, a_sink, b_lambda, b_subln, c_qnorm, c_knorm):
    batch, seq, d = x_prompt.shape
    lat_batch, lat_seq, _ = x_sample.shape
    depth = w_mod.shape[0]
    past = cache_a_k.shape[2]
    n_ctx_rows = batch * seq
    assert d == D_MODEL and depth == DEPTH and w_in.shape[-1] == IN_WIDTH
    assert n_ctx_rows % TM == 0 and lat_seq % TM == 0 and 1 + lat_batch <= N_COND
    assert n_ctx_rows % lat_seq == 0 and lat_seq % TQ_LAT == 0

    x = jnp.concatenate([x_prompt.reshape(n_ctx_rows, d), x_sample.reshape(lat_batch * lat_seq, d)], axis=0)
    cond = jnp.concatenate([c_ctx[None, :], c, jnp.zeros((N_COND - 1 - lat_batch, d), F32)], axis=0)
    mod = _modulation(cond, w_mod, b_mod).reshape(depth, N_COND, 3 * N_SUB, d)
    tables = _rope_tables(n_ctx_rows, lat_batch, lat_seq)
    caches = [a.reshape(lat_batch, depth, past, -1) for a in
              (cache_a_k, cache_a_v, cache_b_k, cache_b_v, cache_c_k, cache_c_v)]
    geom = dict(n_ctx_rows=n_ctx_rows, lat_rows=lat_seq)

    new_kv = [[] for _ in range(6)]
    for l in range(depth):
        x = _ffn(x, mod, norm_pre, norm_post, ffn_gate, ffn_up, ffn_down, layer=l, which=0, j=0, **geom)
        p = _in_proj(x, mod, norm_pre, w_in, layer=l, j=1, **geom)
        q, k, v, kcn = _prep(p, tables, c_qnorm, c_knorm, layer=l)
        o_ctx = _ctx_attention(q, k, v, a_sink, b_lambda, b_subln, layer=l, batch=batch, seq=seq)
        o_lat = _lat_attention(q, k, v, caches, a_sink, b_lambda, b_subln, layer=l,
                               n_ctx_rows=n_ctx_rows, lat_batch=lat_batch, lat_seq=lat_seq)
        o = jnp.concatenate([o_ctx, o_lat], axis=0)
        x = _out_proj(o, x, mod, norm_post, w_out, layer=l, j=1, **geom)
        x = _ffn(x, mod, norm_pre, norm_post, ffn_gate, ffn_up, ffn_down, layer=l, which=1, j=2, **geom)

        pc = p[:n_ctx_rows]
        for lst, name, heads in ((new_kv[0], "ka", A_KV), (new_kv[1], "va", A_KV),
                                 (new_kv[2], "kb", B_HEADS), (new_kv[3], "vb", B_HEADS),
                                 (new_kv[5], "vc", C_KV)):
            lst.append(pc[:, _OFF[name]:_OFF[name] + heads * HEAD_DIM].reshape(batch, seq, heads, HEAD_DIM))
        new_kv[4].append(kcn[:n_ctx_rows].reshape(batch, seq, C_KV, HEAD_DIM))

    y_prompt = x[:n_ctx_rows].reshape(batch, seq, d)
    y_sample = x[n_ctx_rows:].reshape(lat_batch, lat_seq, d)
    return (y_prompt, y_sample) + tuple(jnp.stack(t, axis=1) for t in new_kv)
```

```python
import functools
import math

import jax
import jax.numpy as jnp
import numpy as np
from jax import lax
from jax.experimental import pallas as pl
from jax.experimental.pallas import tpu as pltpu

D_MODEL = 2048
DEPTH = 4
GRID_W = 64
HEAD_DIM = 128
A_HEADS, A_KV = 6, 2
B_HEADS, B_QK_DIM = 4, 64
C_HEADS, C_KV = 6, 2
D_FF = 5632
WINDOW = 128
ROPE_THETA = 10000.0
EPS = 1e-6
FFN_RESID = 0.5
N_SUB = 3
N_COND = 8
Q_WIDTH = (A_HEADS + B_HEADS + C_HEADS) * HEAD_DIM
KV_HEADS = A_KV + B_HEADS + C_KV
KV_WIDTH = KV_HEADS * HEAD_DIM
IN_WIDTH = Q_WIDTH + 2 * KV_WIDTH
LOG2_E = math.log2(math.e)
_OFF = {}
_o = 0
for _name, _w in (("qa", A_HEADS), ("ka", A_KV), ("va", A_KV), ("qb", B_HEADS), ("kb", B_HEADS),
                  ("vb", B_HEADS), ("qc", C_HEADS), ("kc", C_KV), ("vc", C_KV)):
    _OFF[_name] = _o
    _o += _w * HEAD_DIM

V7X_VMEM_LIMIT_BYTES = 56 * 1024 * 1024
LANES = 128

TM = 1024
TF = 256
TN_IN = 512
TK_OUT = 512
TN_MOD = 1024
ROW_CHUNK = 32
TR_PREP = 256
TQ_LAT = 256

F32 = jnp.float32
BF16 = jnp.bfloat16


def _rms(x):
    return x * lax.rsqrt(jnp.mean(x * x, axis=-1, keepdims=True) + EPS)


def _dot(a, b):
    return jnp.dot(a, b, preferred_element_type=F32)


def _dot_t(a, b):
    return lax.dot_general(a, b, (((1,), (1,)), ((), ())), preferred_element_type=F32)


def _cond_of_tile(i, tm, n_ctx_rows, lat_rows):
    return jnp.maximum((i * tm - n_ctx_rows) // lat_rows + 1, 0)


def _params(semantics, vmem_bytes):
    return pltpu.CompilerParams(dimension_semantics=semantics, vmem_limit_bytes=vmem_bytes)


def _mod_kernel(cond_ref, w_ref, b_ref, o_ref):
    c = cond_ref[...]
    s = (c * jax.nn.sigmoid(c)).astype(BF16)
    o_ref[...] = _dot(s, w_ref[...].astype(BF16)) + b_ref[...]


def _modulation(cond, w_mod, b_mod):
    depth, d, n = w_mod.shape
    return pl.pallas_call(
        _mod_kernel,
        out_shape=jax.ShapeDtypeStruct((depth, N_COND, n), F32),
        grid=(depth, n // TN_MOD),
        in_specs=[
            pl.BlockSpec((N_COND, d), lambda l, j: (0, 0)),
            pl.BlockSpec((None, d, TN_MOD), lambda l, j: (l, 0, j)),
            pl.BlockSpec((None, 1, TN_MOD), lambda l, j: (l, 0, j)),
        ],
        out_specs=pl.BlockSpec((None, N_COND, TN_MOD), lambda l, j: (l, 0, j)),
        compiler_params=_params(("arbitrary", "arbitrary"), 40 * 1024 * 1024),
        name="adaln_modulation",
    )(cond, w_mod, b_mod.reshape(depth, 1, n))


def _row_chunks(n_rows, body, unroll):
    def step(r, carry):
        body(pl.ds(pl.multiple_of(r * ROW_CHUNK, ROW_CHUNK), ROW_CHUNK))
        return carry

    lax.fori_loop(0, n_rows // ROW_CHUNK, step, 0, unroll=unroll)


def _row_rsqrt_mean_square(src_ref, rs_ref):
    def body(rows):
        x = src_ref[rows, :]
        ms = jnp.mean(x * x, axis=-1, keepdims=True)
        rs_ref[rows, :] = jnp.broadcast_to(lax.rsqrt(ms + EPS), (ROW_CHUNK, LANES))

    _row_chunks(src_ref.shape[0], body, unroll=8)


def _modulated_norm_rows(x_ref, mod_ref, gpre_ref, h_ref, rs_ref):
    _row_rsqrt_mean_square(x_ref, rs_ref)
    shift = mod_ref[0:1, :]
    gain = gpre_ref[...] * (1.0 + mod_ref[1:2, :])

    def body(rows):
        rs = rs_ref[rows, :]
        for t in range(x_ref.shape[1] // LANES):
            cols = slice(t * LANES, (t + 1) * LANES)
            h_ref[rows, cols] = (x_ref[rows, cols] * rs * gain[:, cols] + shift[:, cols]).astype(BF16)

    _row_chunks(x_ref.shape[0], body, unroll=2)


def _gated_residual_rows(x_ref, y_ref, o_ref, mod_ref, gpost_ref, rs_ref, weight):
    _row_rsqrt_mean_square(y_ref, rs_ref)
    gain = (weight * mod_ref[2:3, :]) * gpost_ref[...]

    def body(rows):
        rs = rs_ref[rows, :]
        for t in range(x_ref.shape[1] // LANES):
            cols = slice(t * LANES, (t + 1) * LANES)
            o_ref[rows, cols] = x_ref[rows, cols] + y_ref[rows, cols] * rs * gain[:, cols]

    _row_chunks(x_ref.shape[0], body, unroll=2)


def _sublayer_specs(d, cond, n_grid):
    def mod_map(*args):
        i, s = args[0], args[n_grid]
        return (s[0], cond(i), s[1], 0, 0)

    def gain_map(*args):
        s = args[n_grid]
        return (s[0], s[1], 0, 0)

    return (pl.BlockSpec((None, None, None, 3, d), mod_map),
            pl.BlockSpec((None, None, 1, d), gain_map))


def _ffn_kernel(s_ref, x_ref, mod_ref, gpre_ref, gpost_ref, wg_ref, wu_ref, wd_ref, o_ref, h_ref, rs_ref):
    f = pl.program_id(1)

    @pl.when(f == 0)
    def _():
        _modulated_norm_rows(x_ref, mod_ref, gpre_ref, h_ref, rs_ref)
        o_ref[...] = jnp.zeros_like(o_ref)

    h = h_ref[...]
    g = _dot(h, wg_ref[...].astype(BF16))
    u = _dot(h, wu_ref[...].astype(BF16))
    a = (g * jax.nn.sigmoid(g) * u).astype(BF16)
    wd = wd_ref[...].astype(BF16)
    nchunk = 512
    for n in range(0, D_MODEL, nchunk):
        o_ref[:, n:n + nchunk] += _dot(a, wd[:, n:n + nchunk])

    @pl.when(f == pl.num_programs(1) - 1)
    def _():
        _gated_residual_rows(x_ref, o_ref, o_ref, mod_ref, gpost_ref, rs_ref, FFN_RESID)


def _ffn(sel, x, mod, norm_pre, norm_post, w_gate, w_up, w_down, *, n_ctx_rows, lat_rows):
    m, d = x.shape
    cond = functools.partial(_cond_of_tile, tm=TM, n_ctx_rows=n_ctx_rows, lat_rows=lat_rows)
    mod_spec, gain_spec = _sublayer_specs(d, cond, 2)
    return pl.pallas_call(
        _ffn_kernel,
        out_shape=jax.ShapeDtypeStruct((m, d), F32),
        grid_spec=pltpu.PrefetchScalarGridSpec(
            num_scalar_prefetch=1,
            grid=(m // TM, D_FF // TF),
            in_specs=[
                pl.BlockSpec((TM, d), lambda i, f, s: (i, 0)),
                mod_spec, gain_spec, gain_spec,
                pl.BlockSpec((None, None, d, TF), lambda i, f, s: (s[0], s[2], 0, f)),
                pl.BlockSpec((None, None, d, TF), lambda i, f, s: (s[0], s[2], 0, f)),
                pl.BlockSpec((None, None, TF, d), lambda i, f, s: (s[0], s[2], f, 0)),
            ],
            out_specs=pl.BlockSpec((TM, d), lambda i, f, s: (i, 0)),
            scratch_shapes=[pltpu.VMEM((TM, d), BF16), pltpu.VMEM((TM, LANES), F32)]),
        compiler_params=_params(("arbitrary", "arbitrary"), V7X_VMEM_LIMIT_BYTES),
        name="ffn_sublayer",
    )(sel, x, mod, norm_pre, norm_post, w_gate, w_up, w_down)


def _in_proj_kernel(s_ref, x_ref, mod_ref, gpre_ref, w_ref, o_ref, h_ref, rs_ref):
    @pl.when(pl.program_id(1) == 0)
    def _():
        _modulated_norm_rows(x_ref, mod_ref, gpre_ref, h_ref, rs_ref)

    o_ref[...] = _dot(h_ref[...], w_ref[...].astype(BF16))


def _in_proj(sel, x, mod, norm_pre, w_in, *, n_ctx_rows, lat_rows):
    m, d = x.shape
    n = w_in.shape[-1]
    cond = functools.partial(_cond_of_tile, tm=TM, n_ctx_rows=n_ctx_rows, lat_rows=lat_rows)
    mod_spec, gain_spec = _sublayer_specs(d, cond, 2)
    return pl.pallas_call(
        _in_proj_kernel,
        out_shape=jax.ShapeDtypeStruct((m, n), F32),
        grid_spec=pltpu.PrefetchScalarGridSpec(
            num_scalar_prefetch=1,
            grid=(m // TM, n // TN_IN),
            in_specs=[
                pl.BlockSpec((TM, d), lambda i, c, s: (i, 0)),
                mod_spec, gain_spec,
                pl.BlockSpec((None, d, TN_IN), lambda i, c, s: (s[0], 0, c)),
            ],
            out_specs=pl.BlockSpec((TM, TN_IN), lambda i, c, s: (i, c)),
            scratch_shapes=[pltpu.VMEM((TM, d), BF16), pltpu.VMEM((TM, LANES), F32)]),
        compiler_params=_params(("arbitrary", "arbitrary"), 48 * 1024 * 1024),
        name="qkv_projection",
    )(sel, x, mod, norm_pre, w_in)


def _rope128(x, cos, sin):
    return x * cos + pltpu.roll(x, HEAD_DIM // 2, axis=1) * sin


def _rope64(x, cos, sin, low_half):
    swapped = jnp.where(low_half, pltpu.roll(x, LANES - B_QK_DIM // 2, axis=1),
                        pltpu.roll(x, B_QK_DIM // 2, axis=1))
    return x * cos + swapped * sin


def _prep_kernel(s_ref, p_ref, ca_ref, sa_ref, cb_ref, sb_ref, qn_ref, kn_ref,
                 q_ref, k_ref, v_ref, kcn_ref):
    ca, sa = ca_ref[...], sa_ref[...]
    cb, sb = cb_ref[...], sb_ref[...]
    qn = qn_ref[...]
    kn = kn_ref[...]
    lane = lax.broadcasted_iota(jnp.int32, (1, LANES), 1)
    low_half = (lane % B_QK_DIM) < (B_QK_DIM // 2)
    scale_hd = HEAD_DIM ** -0.5 * LOG2_E
    scale_qk = B_QK_DIM ** -0.5 * LOG2_E

    def head(name, i):
        c0 = _OFF[name] + i * HEAD_DIM
        return p_ref[:, c0:c0 + HEAD_DIM]

    def put(ref, col_head, val):
        ref[:, col_head * HEAD_DIM:(col_head + 1) * HEAD_DIM] = val.astype(ref.dtype)

    for i in range(A_HEADS):
        put(q_ref, i, _rope128(head("qa", i), ca, sa) * scale_hd)
    for i in range(B_HEADS):
        put(q_ref, A_HEADS + i, _rope64(head("qb", i), cb, sb, low_half) * scale_qk)
    for i in range(C_HEADS):
        put(q_ref, A_HEADS + B_HEADS + i, _rope128(_rms(head("qc", i)) * qn, ca, sa) * scale_hd)
    for i in range(A_KV):
        put(k_ref, i, _rope128(head("ka", i), ca, sa))
        put(v_ref, i, head("va", i))
    for i in range(B_HEADS):
        put(k_ref, A_KV + i, _rope64(head("kb", i), cb, sb, low_half))
        put(v_ref, A_KV + i, head("vb", i))
    for i in range(C_KV):
        kc = _rms(head("kc", i)) * kn
        put(kcn_ref, i, kc)
        put(k_ref, A_KV + B_HEADS + i, _rope128(kc, ca, sa))
        put(v_ref, A_KV + B_HEADS + i, head("vc", i))


def _prep(sel, p, tables, c_qnorm, c_knorm):
    m = p.shape[0]
    row = lambda w: pl.BlockSpec((TR_PREP, w), lambda i, s: (i, 0))
    gain = pl.BlockSpec((None, 1, HEAD_DIM), lambda i, s: (s[0], 0, 0))
    return pl.pallas_call(
        _prep_kernel,
        out_shape=(jax.ShapeDtypeStruct((m, Q_WIDTH), BF16),
                   jax.ShapeDtypeStruct((m, KV_WIDTH), BF16),
                   jax.ShapeDtypeStruct((m, KV_WIDTH), BF16),
                   jax.ShapeDtypeStruct((m, C_KV * HEAD_DIM), F32)),
        grid_spec=pltpu.PrefetchScalarGridSpec(
            num_scalar_prefetch=1,
            grid=(m // TR_PREP,),
            in_specs=[row(IN_WIDTH), row(LANES), row(LANES), row(LANES), row(LANES), gain, gain],
            out_specs=(row(Q_WIDTH), row(KV_WIDTH), row(KV_WIDTH), row(C_KV * HEAD_DIM))),
        compiler_params=_params(("arbitrary",), 40 * 1024 * 1024),
        name="qkv_prepare",
    )(sel, p, *tables, c_qnorm, c_knorm)


VX = 2 * HEAD_DIM


def _store_values_with_ones(v, vx_ref, n_heads, head0=0):
    ones = jnp.ones((v.shape[0], HEAD_DIM), BF16)
    for h in range(n_heads):
        c0 = (head0 + h) * VX
        vx_ref[:, c0:c0 + HEAD_DIM] = v[:, h * HEAD_DIM:(h + 1) * HEAD_DIM]
        vx_ref[:, c0 + HEAD_DIM:c0 + VX] = ones


def _key_loader(ref, rows, head0):
    return lambda h: ref[rows, (head0 + h) * HEAD_DIM:(head0 + h + 1) * HEAD_DIM]


def _value_loader(ref, rows, head0):
    return lambda h: ref[rows, (head0 + h) * VX:(head0 + h + 1) * VX]


_KV_HEAD0 = {"a": 0, "b": A_KV, "c": A_KV + B_HEADS}


def _exp_weighted(scores, values, floor=None):
    m = functools.reduce(jnp.maximum, [jnp.max(s, axis=-1, keepdims=True) for s in scores])
    if floor is not None:
        m = jnp.maximum(m, floor)
    acc = functools.reduce(lambda a, b: a + b,
                           [_dot(jnp.exp2(s - m).astype(BF16), v) for s, v in zip(scores, values)])
    return acc, m


def _attend(q_ref, segs, o_ref, sink_ref, lam_init_ref, bl_ref, subln_ref, layer, local_mask=None):
    lam_init = lam_init_ref[layer]
    bl = bl_ref[...]
    s1 = jnp.sum(bl[0:1, :] * bl[1:2, :], axis=-1, keepdims=True)
    s2 = jnp.sum(bl[2:3, :] * bl[3:4, :], axis=-1, keepdims=True)
    lam = jnp.exp(s1) - jnp.exp(s2) + lam_init
    g_sub = subln_ref[...] * (1.0 - lam_init)
    lane = lax.broadcasted_iota(jnp.int32, (1, HEAD_DIM), 1)
    first_map = lane < B_QK_DIM

    def split(acc):
        return acc[:, :HEAD_DIM], acc[:, HEAD_DIM:]

    for i in range(A_HEADS):
        kv = i // (A_HEADS // A_KV)
        q = q_ref[:, i * HEAD_DIM:(i + 1) * HEAD_DIM]
        scores = [_dot_t(q, k(kv)) for k, _ in segs["a"]]
        if local_mask is not None:
            scores[-1] = jnp.where(local_mask, scores[-1], -jnp.inf)
        sink = sink_ref[layer, i] * LOG2_E
        acc, m = _exp_weighted(scores, [v(kv) for _, v in segs["a"]], floor=sink)
        num, den = split(acc)
        o = num / (den + jnp.exp2(sink - m))
        o_ref[:, i * HEAD_DIM:(i + 1) * HEAD_DIM] = o.astype(o_ref.dtype)

    for i in range(B_HEADS):
        c0 = (A_HEADS + i) * HEAD_DIM
        q = q_ref[:, c0:c0 + HEAD_DIM]
        q1 = jnp.where(first_map, q, jnp.zeros_like(q))
        q2 = jnp.where(first_map, jnp.zeros_like(q), q)
        values = [v(i) for _, v in segs["b"]]
        n1, d1 = split(_exp_weighted([_dot_t(q1, k(i)) for k, _ in segs["b"]], values)[0])
        n2, d2 = split(_exp_weighted([_dot_t(q2, k(i)) for k, _ in segs["b"]], values)[0])
        o = n1 / d1 - lam * (n2 / d2)
        o = _rms(o) * g_sub
        o_ref[:, c0:c0 + HEAD_DIM] = o.astype(o_ref.dtype)

    for i in range(C_HEADS):
        kv = i // (C_HEADS // C_KV)
        c0 = (A_HEADS + B_HEADS + i) * HEAD_DIM
        q = q_ref[:, c0:c0 + HEAD_DIM]
        acc, _ = _exp_weighted([_dot_t(q, k(kv)) for k, _ in segs["c"]], [v(kv) for _, v in segs["c"]])
        num, den = split(acc)
        o_ref[:, c0:c0 + HEAD_DIM] = (num / den).astype(o_ref.dtype)


def _ctx_attn_kernel(s_ref, q_ref, k_ref, v_ref, sink_ref, lam_init_ref, bl_ref, subln_ref, o_ref, vx_ref):
    _store_values_with_ones(v_ref[...], vx_ref, KV_HEADS)
    rows = slice(None)
    segs = {g: [(_key_loader(k_ref, rows, h0), _value_loader(vx_ref, rows, h0))]
            for g, h0 in _KV_HEAD0.items()}
    _attend(q_ref, segs, o_ref, sink_ref, lam_init_ref, bl_ref, subln_ref, s_ref[0])


def _attn_param_specs(n_grid):
    def layer_map(*args):
        return (args[n_grid][0], 0, 0)

    smem = pl.BlockSpec(memory_space=pltpu.SMEM)
    return [smem, smem,
            pl.BlockSpec((None, 4, B_QK_DIM), layer_map),
            pl.BlockSpec((None, 1, HEAD_DIM), layer_map)]


def _ctx_attention(sel, q, k, v, a_sink, lam_init, b_lambda, b_subln, *, batch, seq):
    return pl.pallas_call(
        _ctx_attn_kernel,
        out_shape=jax.ShapeDtypeStruct((batch * seq, Q_WIDTH), BF16),
        grid_spec=pltpu.PrefetchScalarGridSpec(
            num_scalar_prefetch=1,
            grid=(batch,),
            in_specs=[pl.BlockSpec((seq, Q_WIDTH), lambda b, s: (b, 0)),
                      pl.BlockSpec((seq, KV_WIDTH), lambda b, s: (b, 0)),
                      pl.BlockSpec((seq, KV_WIDTH), lambda b, s: (b, 0))] + _attn_param_specs(1),
            out_specs=pl.BlockSpec((seq, Q_WIDTH), lambda b, s: (b, 0)),
            scratch_shapes=[pltpu.VMEM((seq, KV_HEADS * VX), BF16)]),
        compiler_params=_params(("arbitrary",), 40 * 1024 * 1024),
        name="context_attention",
    )(sel, q, k, v, a_sink, lam_init, b_lambda, b_subln)


def _lat_attn_kernel(s_ref, q_ref, k_ref, v_ref, cak_ref, cav_ref, cbk_ref, cbv_ref, cck_ref, ccv_ref,
                     sink_ref, lam_init_ref, bl_ref, subln_ref, o_ref, ck_ref, cvx_ref, vx_ref, *, lat_seq):
    @pl.when(pl.program_id(1) == 0)
    def _():
        h0 = 0
        for kref, vref in ((cak_ref, cav_ref), (cbk_ref, cbv_ref), (cck_ref, ccv_ref)):
            n_heads = kref.shape[-1] // HEAD_DIM
            ck_ref[:, h0 * HEAD_DIM:(h0 + n_heads) * HEAD_DIM] = kref[...].astype(BF16)
            _store_values_with_ones(vref[...].astype(BF16), cvx_ref, n_heads, h0)
            h0 += n_heads
        _store_values_with_ones(v_ref[...], vx_ref, KV_HEADS)

    tq = q_ref.shape[0]
    band = tq + 2 * WINDOW
    q0 = pl.program_id(1) * tq
    k0 = pl.multiple_of(jnp.clip(q0 - WINDOW, 0, lat_seq - band), WINDOW)
    qpos = q0 + lax.broadcasted_iota(jnp.int32, (tq, 1), 0)
    kpos = k0 + lax.broadcasted_iota(jnp.int32, (1, band), 1)
    local_mask = jnp.abs(qpos - kpos) <= WINDOW

    every = slice(None)
    lat_rows = {"a": pl.ds(k0, band), "b": every, "c": every}
    segs = {g: [(_key_loader(ck_ref, every, h0), _value_loader(cvx_ref, every, h0)),
                (_key_loader(k_ref, lat_rows[g], h0), _value_loader(vx_ref, lat_rows[g], h0))]
            for g, h0 in _KV_HEAD0.items()}
    _attend(q_ref, segs, o_ref, sink_ref, lam_init_ref, bl_ref, subln_ref, s_ref[0], local_mask=local_mask)


def _lat_attention(sel, q, k, v, caches, a_sink, lam_init, b_lambda, b_subln, *, n_ctx_rows, lat_batch, lat_seq):
    q_blk0 = n_ctx_rows // TQ_LAT
    kv_blk0 = n_ctx_rows // lat_seq
    n_q = lat_seq // TQ_LAT
    past = caches[0].shape[2]

    def cache_spec(a):
        return pl.BlockSpec((None, None) + a.shape[2:], lambda b, i, s: (b, s[0], 0, 0))

    return pl.pallas_call(
        functools.partial(_lat_attn_kernel, lat_seq=lat_seq),
        out_shape=jax.ShapeDtypeStruct((lat_batch * lat_seq, Q_WIDTH), BF16),
        grid_spec=pltpu.PrefetchScalarGridSpec(
            num_scalar_prefetch=1,
            grid=(lat_batch, n_q),
            in_specs=[pl.BlockSpec((TQ_LAT, Q_WIDTH), lambda b, i, s: (q_blk0 + b * n_q + i, 0)),
                      pl.BlockSpec((lat_seq, KV_WIDTH), lambda b, i, s: (kv_blk0 + b, 0)),
                      pl.BlockSpec((lat_seq, KV_WIDTH), lambda b, i, s: (kv_blk0 + b, 0))]
                     + [cache_spec(a) for a in caches] + _attn_param_specs(2),
            out_specs=pl.BlockSpec((TQ_LAT, Q_WIDTH), lambda b, i, s: (b * n_q + i, 0)),
            scratch_shapes=[pltpu.VMEM((past, KV_WIDTH), BF16),
                            pltpu.VMEM((past, KV_HEADS * VX), BF16),
                            pltpu.VMEM((lat_seq, KV_HEADS * VX), BF16)]),
        compiler_params=_params(("arbitrary", "arbitrary"), V7X_VMEM_LIMIT_BYTES),
        name="latent_attention",
    )(sel, q, k, v, *caches, a_sink, lam_init, b_lambda, b_subln)


def _out_proj_kernel(s_ref, a_ref, x_ref, mod_ref, gpost_ref, w_ref, o_ref, rs_ref):
    kk = pl.program_id(1)

    @pl.when(kk == 0)
    def _():
        o_ref[...] = jnp.zeros_like(o_ref)

    o_ref[...] += _dot(a_ref[...], w_ref[...].astype(BF16))

    @pl.when(kk == pl.num_programs(1) - 1)
    def _():
        _gated_residual_rows(x_ref, o_ref, o_ref, mod_ref, gpost_ref, rs_ref, 1.0)


def _out_proj(sel, a, x, mod, norm_post, w_out, *, n_ctx_rows, lat_rows):
    m, d = x.shape
    kdim = a.shape[1]
    cond = functools.partial(_cond_of_tile, tm=TM, n_ctx_rows=n_ctx_rows, lat_rows=lat_rows)
    mod_spec, gain_spec = _sublayer_specs(d, cond, 2)
    return pl.pallas_call(
        _out_proj_kernel,
        out_shape=jax.ShapeDtypeStruct((m, d), F32),
        grid_spec=pltpu.PrefetchScalarGridSpec(
            num_scalar_prefetch=1,
            grid=(m // TM, kdim // TK_OUT),
            in_specs=[
                pl.BlockSpec((TM, TK_OUT), lambda i, k, s: (i, k)),
                pl.BlockSpec((TM, d), lambda i, k, s: (i, 0)),
                mod_spec, gain_spec,
                pl.BlockSpec((None, TK_OUT, d), lambda i, k, s: (s[0], k, 0)),
            ],
            out_specs=pl.BlockSpec((TM, d), lambda i, k, s: (i, 0)),
            scratch_shapes=[pltpu.VMEM((TM, LANES), F32)]),
        compiler_params=_params(("arbitrary", "arbitrary"), V7X_VMEM_LIMIT_BYTES),
        name="out_projection",
    )(sel, a, x, mod, norm_post, w_out)


def _rope_tables(n_ctx_rows, lat_batch, lat_seq):
    t = np.arange(lat_seq)
    row = (t // GRID_W).astype(np.float32)
    col = (t % GRID_W).astype(np.float32)

    def tables(dim):
        nf = dim // 4
        inv = jnp.asarray(ROPE_THETA, F32) ** (-jnp.arange(nf, dtype=F32) / nf)
        ang = jnp.concatenate([row[:, None] * inv, col[:, None] * inv], axis=-1)
        cos, sin = jnp.cos(ang), jnp.sin(ang)
        cos_t = jnp.tile(jnp.concatenate([cos, cos], axis=-1), (1, LANES // dim))
        sin_t = jnp.tile(jnp.concatenate([-sin, sin], axis=-1), (1, LANES // dim))
        return (jnp.concatenate([jnp.ones((n_ctx_rows, LANES), F32)] + [cos_t] * lat_batch, axis=0),
                jnp.concatenate([jnp.zeros((n_ctx_rows, LANES), F32)] + [sin_t] * lat_batch, axis=0))

    ca, sa = tables(HEAD_DIM)
    cb, sb = tables(B_QK_DIM)
    return ca, sa, cb, sb


def kernel(x_prompt, x_sample, cache_a_k, cache_a_v, cache_b_k, cache_b_v, cache_c_k, cache_c_v,
           c, c_ctx, w_mod, b_mod, norm_pre, norm_post, ffn_gate, ffn_up, ffn_down,
           w_in, w_out, a_sink, b_lambda, b_subln, c_qnorm, c_knorm):
    batch, seq, d = x_prompt.shape
    lat_batch, lat_seq, _ = x_sample.shape
    depth = w_mod.shape[0]
    past = cache_a_k.shape[2]
    n_ctx_rows = batch * seq
    assert d == D_MODEL and depth == DEPTH and w_in.shape[-1] == IN_WIDTH
    assert n_ctx_rows % TM == 0 and lat_seq % TM == 0 and 1 + lat_batch <= N_COND
    assert n_ctx_rows % lat_seq == 0 and lat_seq % TQ_LAT == 0

    x = jnp.concatenate([x_prompt.reshape(n_ctx_rows, d), x_sample.reshape(lat_batch * lat_seq, d)], axis=0)
    cond = jnp.concatenate([c_ctx[None, :], c, jnp.zeros((N_COND - 1 - lat_batch, d), F32)], axis=0)
    mod = _modulation(cond, w_mod, b_mod).reshape(depth, N_COND, N_SUB, 3, d)
    gpre = norm_pre.reshape(depth, N_SUB, 1, d)
    gpost = norm_post.reshape(depth, N_SUB, 1, d)
    tables = _rope_tables(n_ctx_rows, lat_batch, lat_seq)
    caches = [a.reshape(lat_batch, depth, past, -1) for a in
              (cache_a_k, cache_a_v, cache_b_k, cache_b_v, cache_c_k, cache_c_v)]
    lam_init = jnp.asarray([0.8 - 0.6 * math.exp(-0.3 * l) for l in range(depth)], F32)
    subln = b_subln.reshape(depth, 1, HEAD_DIM)
    qnorm = c_qnorm.reshape(depth, 1, HEAD_DIM)
    knorm = c_knorm.reshape(depth, 1, HEAD_DIM)
    geom = dict(n_ctx_rows=n_ctx_rows, lat_rows=lat_seq)

    new_kv = [[] for _ in range(6)]
    for l in range(depth):
        sel = lambda sub, which=0: jnp.asarray([l, sub, which], jnp.int32)
        x = _ffn(sel(0, 0), x, mod, gpre, gpost, ffn_gate, ffn_up, ffn_down, **geom)
        p = _in_proj(sel(1), x, mod, gpre, w_in, **geom)
        q, k, v, kcn = _prep(sel(1), p, tables, qnorm, knorm)
        o_ctx = _ctx_attention(sel(1), q, k, v, a_sink, lam_init, b_lambda, subln, batch=batch, seq=seq)
        o_lat = _lat_attention(sel(1), q, k, v, caches, a_sink, lam_init, b_lambda, subln,
                               n_ctx_rows=n_ctx_rows, lat_batch=lat_batch, lat_seq=lat_seq)
        o = jnp.concatenate([o_ctx, o_lat], axis=0)
        x = _out_proj(sel(1), o, x, mod, gpost, w_out, **geom)
        x = _ffn(sel(2, 1), x, mod, gpre, gpost, ffn_gate, ffn_up, ffn_down, **geom)

        pc = p[:n_ctx_rows]
        for lst, name, heads in ((new_kv[0], "ka", A_KV), (new_kv[1], "va", A_KV),
                                 (new_kv[2], "kb", B_HEADS), (new_kv[3], "vb", B_HEADS),
                                 (new_kv[5], "vc", C_KV)):
            lst.append(pc[:, _OFF[name]:_OFF[name] + heads * HEAD_DIM].reshape(batch, seq, heads, HEAD_DIM))
        new_kv[4].append(kcn[:n_ctx_rows].reshape(batch, seq, C_KV, HEAD_DIM))

    y_prompt = x[:n_ctx_rows].reshape(batch, seq, d)
    y_sample = x[n_ctx_rows:].reshape(lat_batch, lat_seq, d)
    return (y_prompt, y_sample) + tuple(jnp.stack(t, axis=1) for t in new_kv)
```

```python
import functools
import math

import jax
import jax.numpy as jnp
import numpy as np
from jax import lax
from jax.experimental import pallas as pl
from jax.experimental.pallas import tpu as pltpu

D_MODEL = 2048
DEPTH = 4
GRID_W = 64
HEAD_DIM = 128
A_HEADS, A_KV = 6, 2
B_HEADS, B_QK_DIM = 4, 64
C_HEADS, C_KV = 6, 2
D_FF = 5632
WINDOW = 128
ROPE_THETA = 10000.0
EPS = 1e-6
FFN_RESID = 0.5
N_SUB = 3
N_COND = 8
Q_WIDTH = (A_HEADS + B_HEADS + C_HEADS) * HEAD_DIM
KV_HEADS = A_KV + B_HEADS + C_KV
KV_WIDTH = KV_HEADS * HEAD_DIM
IN_WIDTH = Q_WIDTH + 2 * KV_WIDTH
LOG2_E = math.log2(math.e)
_OFF = {}
_o = 0
for _name, _w in (("qa", A_HEADS), ("ka", A_KV), ("va", A_KV), ("qb", B_HEADS), ("kb", B_HEADS),
                  ("vb", B_HEADS), ("qc", C_HEADS), ("kc", C_KV), ("vc", C_KV)):
    _OFF[_name] = _o
    _o += _w * HEAD_DIM

V7X_VMEM_LIMIT_BYTES = 56 * 1024 * 1024
LANES = 128

TM = 1024
TF = 256
TN_IN = 256
TN_OUT = 512
TN_MOD = 1024
ROW_CHUNK = 32
TQ_LAT = 256

F32 = jnp.float32
BF16 = jnp.bfloat16


def _rms(x):
    return x * lax.rsqrt(jnp.mean(x * x, axis=-1, keepdims=True) + EPS)


def _dot(a, b):
    return jnp.dot(a, b, preferred_element_type=F32)


def _dot_t(a, b):
    return lax.dot_general(a, b, (((1,), (1,)), ((), ())), preferred_element_type=F32)


def _cond_of_tile(i, tm, n_ctx_rows, lat_rows):
    return jnp.maximum((i * tm - n_ctx_rows) // lat_rows + 1, 0)


def _params(semantics, vmem_bytes):
    return pltpu.CompilerParams(dimension_semantics=semantics, vmem_limit_bytes=vmem_bytes)


def _mod_kernel(cond_ref, w_ref, b_ref, o_ref):
    c = cond_ref[...]
    s = (c * jax.nn.sigmoid(c)).astype(BF16)
    o_ref[...] = _dot(s, w_ref[...].astype(BF16)) + b_ref[...]


def _modulation(cond, w_mod, b_mod):
    depth, d, n = w_mod.shape
    return pl.pallas_call(
        _mod_kernel,
        out_shape=jax.ShapeDtypeStruct((depth, N_COND, n), F32),
        grid=(depth, n // TN_MOD),
        in_specs=[
            pl.BlockSpec((N_COND, d), lambda l, j: (0, 0)),
            pl.BlockSpec((None, d, TN_MOD), lambda l, j: (l, 0, j)),
            pl.BlockSpec((None, 1, TN_MOD), lambda l, j: (l, 0, j)),
        ],
        out_specs=pl.BlockSpec((None, N_COND, TN_MOD), lambda l, j: (l, 0, j)),
        compiler_params=_params(("arbitrary", "arbitrary"), 40 * 1024 * 1024),
        name="adaln_modulation",
    )(cond, w_mod, b_mod.reshape(depth, 1, n))


def _row_chunks(n_rows, body, unroll):
    def step(r, carry):
        body(pl.ds(pl.multiple_of(r * ROW_CHUNK, ROW_CHUNK), ROW_CHUNK))
        return carry

    lax.fori_loop(0, n_rows // ROW_CHUNK, step, 0, unroll=unroll)


def _row_rsqrt_mean_square(src_ref, rs_ref):
    def body(rows):
        x = src_ref[rows, :]
        ms = jnp.mean(x * x, axis=-1, keepdims=True)
        rs_ref[rows, :] = jnp.broadcast_to(lax.rsqrt(ms + EPS), (ROW_CHUNK, LANES))

    _row_chunks(src_ref.shape[0], body, unroll=8)


def _modulated_norm_rows(x_ref, mod_ref, gpre_ref, h_ref, rs_ref):
    _row_rsqrt_mean_square(x_ref, rs_ref)
    shift = mod_ref[0:1, :]
    gain = gpre_ref[...] * (1.0 + mod_ref[1:2, :])

    def body(rows):
        rs = rs_ref[rows, :]
        for t in range(x_ref.shape[1] // LANES):
            cols = slice(t * LANES, (t + 1) * LANES)
            h_ref[rows, cols] = (x_ref[rows, cols] * rs * gain[:, cols] + shift[:, cols]).astype(BF16)

    _row_chunks(x_ref.shape[0], body, unroll=2)


def _gated_residual_rows(x_ref, y_ref, o_ref, mod_ref, gpost_ref, rs_ref, weight):
    _row_rsqrt_mean_square(y_ref, rs_ref)
    gain = (weight * mod_ref[2:3, :]) * gpost_ref[...]

    def body(rows):
        rs = rs_ref[rows, :]
        for t in range(x_ref.shape[1] // LANES):
            cols = slice(t * LANES, (t + 1) * LANES)
            o_ref[rows, cols] = x_ref[rows, cols] + y_ref[rows, cols] * rs * gain[:, cols]

    _row_chunks(x_ref.shape[0], body, unroll=2)


def _sublayer_specs(d, cond, n_grid):
    def mod_map(*args):
        i, s = args[0], args[n_grid]
        return (s[0], cond(i), s[1], 0, 0)

    def gain_map(*args):
        s = args[n_grid]
        return (s[0], s[1], 0, 0)

    return (pl.BlockSpec((None, None, None, 3, d), mod_map),
            pl.BlockSpec((None, None, 1, d), gain_map))


def _ffn_kernel(s_ref, x_ref, mod_ref, gpre_ref, gpost_ref, wg_ref, wu_ref, wd_ref, o_ref, h_ref, rs_ref):
    f = pl.program_id(1)

    @pl.when(f == 0)
    def _():
        _modulated_norm_rows(x_ref, mod_ref, gpre_ref, h_ref, rs_ref)
        o_ref[...] = jnp.zeros_like(o_ref)

    h = h_ref[...]
    g = _dot(h, wg_ref[...].astype(BF16))
    u = _dot(h, wu_ref[...].astype(BF16))
    a = (g * jax.nn.sigmoid(g) * u).astype(BF16)
    wd = wd_ref[...].astype(BF16)
    nchunk = 512
    for n in range(0, D_MODEL, nchunk):
        o_ref[:, n:n + nchunk] += _dot(a, wd[:, n:n + nchunk])

    @pl.when(f == pl.num_programs(1) - 1)
    def _():
        _gated_residual_rows(x_ref, o_ref, o_ref, mod_ref, gpost_ref, rs_ref, FFN_RESID)


def _ffn(sel, x, mod, norm_pre, norm_post, w_gate, w_up, w_down, *, n_ctx_rows, lat_rows):
    m, d = x.shape
    cond = functools.partial(_cond_of_tile, tm=TM, n_ctx_rows=n_ctx_rows, lat_rows=lat_rows)
    mod_spec, gain_spec = _sublayer_specs(d, cond, 2)
    return pl.pallas_call(
        _ffn_kernel,
        out_shape=jax.ShapeDtypeStruct((m, d), F32),
        grid_spec=pltpu.PrefetchScalarGridSpec(
            num_scalar_prefetch=1,
            grid=(m // TM, D_FF // TF),
            in_specs=[
                pl.BlockSpec((TM, d), lambda i, f, s: (i, 0)),
                mod_spec, gain_spec, gain_spec,
                pl.BlockSpec((None, None, d, TF), lambda i, f, s: (s[0], s[2], 0, f)),
                pl.BlockSpec((None, None, d, TF), lambda i, f, s: (s[0], s[2], 0, f)),
                pl.BlockSpec((None, None, TF, d), lambda i, f, s: (s[0], s[2], f, 0)),
            ],
            out_specs=pl.BlockSpec((TM, d), lambda i, f, s: (i, 0)),
            scratch_shapes=[pltpu.VMEM((TM, d), BF16), pltpu.VMEM((TM, LANES), F32)]),
        compiler_params=_params(("arbitrary", "arbitrary"), V7X_VMEM_LIMIT_BYTES),
        name="ffn_sublayer",
    )(sel, x, mod, norm_pre, norm_post, w_gate, w_up, w_down)


_ROPE_NONE, _ROPE_HD, _ROPE_QK = 0, 1, 2
_NORM_NONE, _NORM_Q, _NORM_K = 0, 1, 2
_SCALE_ONE, _SCALE_HD, _SCALE_QK = 0, 1, 2
N_KV_STEPS = 2 * KV_WIDTH // TN_IN


def _qkv_steps():
    blk = lambda name, j: _OFF[name] // TN_IN + j
    q0, k0, v0 = 0, Q_WIDTH // TN_IN, (Q_WIDTH + KV_WIDTH) // TN_IN
    kv = [(blk("ka", 0), k0 + 0, _ROPE_HD, _NORM_NONE), (blk("va", 0), v0 + 0, _ROPE_NONE, _NORM_NONE),
          (blk("kb", 0), k0 + 1, _ROPE_QK, _NORM_NONE), (blk("kb", 1), k0 + 2, _ROPE_QK, _NORM_NONE),
          (blk("vb", 0), v0 + 1, _ROPE_NONE, _NORM_NONE), (blk("vb", 1), v0 + 2, _ROPE_NONE, _NORM_NONE),
          (blk("kc", 0), k0 + 3, _ROPE_HD, _NORM_K), (blk("vc", 0), v0 + 3, _ROPE_NONE, _NORM_NONE)]
    rows = [(src, dst, j, rope, norm, _SCALE_ONE) for j, (src, dst, rope, norm) in enumerate(kv)]
    last = len(kv) - 1
    qs = ([(blk("qa", j), q0 + j, _ROPE_HD, _NORM_NONE, _SCALE_HD) for j in range(3)]
          + [(blk("qb", j), q0 + 3 + j, _ROPE_QK, _NORM_NONE, _SCALE_QK) for j in range(2)]
          + [(blk("qc", j), q0 + 5 + j, _ROPE_HD, _NORM_Q, _SCALE_HD) for j in range(3)])
    rows += [(src, dst, last, rope, norm, scale) for src, dst, rope, norm, scale in qs]
    return np.asarray(rows, np.int32)


_QKV_STEPS = _qkv_steps()
_ST_SRC, _ST_DST, _ST_KVC, _ST_ROPE, _ST_NORM, _ST_SCALE = range(6)
N_QKV_STEPS = len(_QKV_STEPS)
QKV_SUB_ROWS = 256


def _in_proj_kernel(s_ref, st_ref, x_ref, mod_ref, gpre_ref, w_ref, ca_ref, sa_ref, cb_ref, sb_ref,
                    qn_ref, kn_ref, qkv_ref, kvc_ref, h_ref, rs_ref, n_ref, pa_ref, pb_ref, *, n_ctx_tiles):
    i, c = pl.program_id(0), pl.program_id(1)

    @pl.when(c == 0)
    def _():
        _modulated_norm_rows(x_ref, mod_ref, gpre_ref, h_ref, rs_ref)

    def project(p_ref):
        p_ref[...] = _dot(h_ref[...], w_ref[...].astype(BF16))

    b = jnp.maximum(c - 1, 0)
    rope, norm, scale_kind = st_ref[b, _ST_ROPE], st_ref[b, _ST_NORM], st_ref[b, _ST_SCALE]

    def finish(p_ref, rope_kind, use_norm):
        gain = jnp.where(norm == _NORM_Q, qn_ref[...], kn_ref[...])
        scale = jnp.where(scale_kind == _SCALE_HD, HEAD_DIM ** -0.5 * LOG2_E,
                          jnp.where(scale_kind == _SCALE_QK, B_QK_DIM ** -0.5 * LOG2_E, 1.0))
        lane = lax.broadcasted_iota(jnp.int32, (1, LANES), 1)
        low_half = (lane % B_QK_DIM) < (B_QK_DIM // 2)
        cos_ref, sin_ref = {_ROPE_NONE: (None, None), _ROPE_HD: (ca_ref, sa_ref),
                            _ROPE_QK: (cb_ref, sb_ref)}[rope_kind]
        for r in range(p_ref.shape[0] // QKV_SUB_ROWS):
            rows = slice(r * QKV_SUB_ROWS, (r + 1) * QKV_SUB_ROWS)
            for hh in range(TN_IN // HEAD_DIM):
                cols = slice(hh * HEAD_DIM, (hh + 1) * HEAD_DIM)
                n = p_ref[rows, cols]
                if use_norm:
                    n = _rms(n) * gain
                n_ref[rows, cols] = n
                if rope_kind == _ROPE_HD:
                    partner = pltpu.roll(n, HEAD_DIM // 2, axis=1)
                elif rope_kind == _ROPE_QK:
                    partner = jnp.where(low_half, pltpu.roll(n, LANES - B_QK_DIM // 2, axis=1),
                                        pltpu.roll(n, B_QK_DIM // 2, axis=1))
                if rope_kind != _ROPE_NONE:
                    n = n * cos_ref[rows, :] + partner * sin_ref[rows, :]
                qkv_ref[rows, cols] = (n * scale).astype(BF16)

        @pl.when((b < N_KV_STEPS) & (i < n_ctx_tiles))
        def _():
            kvc_ref[...] = n_ref[...]

    recipes = sorted({(int(r[_ST_ROPE]), int(r[_ST_NORM]) != _NORM_NONE) for r in _QKV_STEPS})

    def finish_by_recipe(when, project_ref, finish_ref):
        for rope_kind, use_norm in recipes:
            @pl.when(when & (rope == rope_kind) & ((norm != _NORM_NONE) == use_norm))
            def _():
                if project_ref is not None:
                    project(project_ref)
                finish(finish_ref, rope_kind, use_norm)

    odd = c % 2 == 1
    inner = (c > 0) & (c < N_QKV_STEPS)

    @pl.when(c == 0)
    def _():
        project(pa_ref)

    finish_by_recipe(inner & odd, pb_ref, pa_ref)
    finish_by_recipe(inner & jnp.logical_not(odd), pa_ref, pb_ref)
    finish_by_recipe(c == N_QKV_STEPS, None, pa_ref if N_QKV_STEPS % 2 == 1 else pb_ref)


def _in_proj(sel, x, mod, norm_pre, w_in, tables, c_qnorm, c_knorm, *, n_ctx_rows, lat_rows):
    m, d = x.shape
    n_ctx_tiles = n_ctx_rows // TM
    cond = functools.partial(_cond_of_tile, tm=TM, n_ctx_rows=n_ctx_rows, lat_rows=lat_rows)
    mod_spec, gain_spec = _sublayer_specs(d, cond, 2)
    table = pl.BlockSpec((TM, LANES), lambda i, c, s, st: (i, 0))
    head_gain = pl.BlockSpec((None, 1, HEAD_DIM), lambda i, c, s, st: (s[0], 0, 0))
    last_kvc = N_KV_STEPS - 1
    projected = lambda c: jnp.minimum(c, N_QKV_STEPS - 1)
    finished = lambda c: jnp.maximum(c - 1, 0)

    def kvc_map(i, c, s, st):
        ctx = i < n_ctx_tiles
        return (jnp.minimum(i, n_ctx_tiles - 1), jnp.where(ctx, st[finished(c), _ST_KVC], last_kvc))

    p_buffer = pltpu.VMEM((TM, TN_IN), F32)
    return pl.pallas_call(
        functools.partial(_in_proj_kernel, n_ctx_tiles=n_ctx_tiles),
        out_shape=(jax.ShapeDtypeStruct((m, IN_WIDTH), BF16),
                   jax.ShapeDtypeStruct((n_ctx_rows, N_KV_STEPS * TN_IN), F32)),
        grid_spec=pltpu.PrefetchScalarGridSpec(
            num_scalar_prefetch=2,
            grid=(m // TM, N_QKV_STEPS + 1),
            in_specs=[
                pl.BlockSpec((TM, d), lambda i, c, s, st: (i, 0)),
                mod_spec, gain_spec,
                pl.BlockSpec((None, d, TN_IN), lambda i, c, s, st: (s[0], 0, st[projected(c), _ST_SRC])),
                table, table, table, table, head_gain, head_gain,
            ],
            out_specs=(pl.BlockSpec((TM, TN_IN), lambda i, c, s, st: (i, st[finished(c), _ST_DST])),
                       pl.BlockSpec((TM, TN_IN), kvc_map)),
            scratch_shapes=[pltpu.VMEM((TM, d), BF16), pltpu.VMEM((TM, LANES), F32),
                            pltpu.VMEM((TM, TN_IN), F32), p_buffer, p_buffer]),
        compiler_params=_params(("arbitrary", "arbitrary"), 48 * 1024 * 1024),
        name="qkv_projection",
    )(sel, jnp.asarray(_QKV_STEPS), x, mod, norm_pre, w_in, *tables, c_qnorm, c_knorm)


VX = 2 * HEAD_DIM


def _store_values_with_ones(v, vx_ref, n_heads, head0=0):
    ones = jnp.ones((v.shape[0], HEAD_DIM), BF16)
    for h in range(n_heads):
        c0 = (head0 + h) * VX
        vx_ref[:, c0:c0 + HEAD_DIM] = v[:, h * HEAD_DIM:(h + 1) * HEAD_DIM]
        vx_ref[:, c0 + HEAD_DIM:c0 + VX] = ones


def _key_loader(ref, rows, head0):
    return lambda h: ref[rows, (head0 + h) * HEAD_DIM:(head0 + h + 1) * HEAD_DIM]


def _value_loader(ref, rows, head0):
    return lambda h: ref[rows, (head0 + h) * VX:(head0 + h + 1) * VX]


_KV_HEAD0 = {"a": 0, "b": A_KV, "c": A_KV + B_HEADS}


def _exp_weighted(scores, values, floor=None):
    m = functools.reduce(jnp.maximum, [jnp.max(s, axis=-1, keepdims=True) for s in scores])
    if floor is not None:
        m = jnp.maximum(m, floor)
    acc = functools.reduce(lambda a, b: a + b,
                           [_dot(jnp.exp2(s - m).astype(BF16), v) for s, v in zip(scores, values)])
    return acc, m


def _attend(q_ref, segs, o_ref, sink_ref, lam_init_ref, bl_ref, subln_ref, layer, local_mask=None):
    lam_init = lam_init_ref[layer]
    bl = bl_ref[...]
    s1 = jnp.sum(bl[0:1, :] * bl[1:2, :], axis=-1, keepdims=True)
    s2 = jnp.sum(bl[2:3, :] * bl[3:4, :], axis=-1, keepdims=True)
    lam = jnp.exp(s1) - jnp.exp(s2) + lam_init
    g_sub = subln_ref[...] * (1.0 - lam_init)
    lane = lax.broadcasted_iota(jnp.int32, (1, HEAD_DIM), 1)
    first_map = lane < B_QK_DIM

    def split(acc):
        return acc[:, :HEAD_DIM], acc[:, HEAD_DIM:]

    for i in range(A_HEADS):
        kv = i // (A_HEADS // A_KV)
        q = q_ref[:, i * HEAD_DIM:(i + 1) * HEAD_DIM]
        scores = [_dot_t(q, k(kv)) for k, _ in segs["a"]]
        if local_mask is not None:
            scores[-1] = jnp.where(local_mask, scores[-1], -jnp.inf)
        sink = sink_ref[layer, i] * LOG2_E
        acc, m = _exp_weighted(scores, [v(kv) for _, v in segs["a"]], floor=sink)
        num, den = split(acc)
        o = num / (den + jnp.exp2(sink - m))
        o_ref[:, i * HEAD_DIM:(i + 1) * HEAD_DIM] = o.astype(o_ref.dtype)

    for i in range(B_HEADS):
        c0 = (A_HEADS + i) * HEAD_DIM
        q = q_ref[:, c0:c0 + HEAD_DIM]
        q1 = jnp.where(first_map, q, jnp.zeros_like(q))
        q2 = jnp.where(first_map, jnp.zeros_like(q), q)
        values = [v(i) for _, v in segs["b"]]
        n1, d1 = split(_exp_weighted([_dot_t(q1, k(i)) for k, _ in segs["b"]], values)[0])
        n2, d2 = split(_exp_weighted([_dot_t(q2, k(i)) for k, _ in segs["b"]], values)[0])
        o = n1 / d1 - lam * (n2 / d2)
        o = _rms(o) * g_sub
        o_ref[:, c0:c0 + HEAD_DIM] = o.astype(o_ref.dtype)

    for i in range(C_HEADS):
        kv = i // (C_HEADS // C_KV)
        c0 = (A_HEADS + B_HEADS + i) * HEAD_DIM
        q = q_ref[:, c0:c0 + HEAD_DIM]
        acc, _ = _exp_weighted([_dot_t(q, k(kv)) for k, _ in segs["c"]], [v(kv) for _, v in segs["c"]])
        num, den = split(acc)
        o_ref[:, c0:c0 + HEAD_DIM] = (num / den).astype(o_ref.dtype)


def _ctx_attn_kernel(s_ref, q_ref, k_ref, v_ref, sink_ref, lam_init_ref, bl_ref, subln_ref, o_ref, vx_ref):
    _store_values_with_ones(v_ref[...], vx_ref, KV_HEADS)
    rows = slice(None)
    segs = {g: [(_key_loader(k_ref, rows, h0), _value_loader(vx_ref, rows, h0))]
            for g, h0 in _KV_HEAD0.items()}
    _attend(q_ref, segs, o_ref, sink_ref, lam_init_ref, bl_ref, subln_ref, s_ref[0])


def _attn_param_specs(n_grid):
    def layer_map(*args):
        return (args[n_grid][0], 0, 0)

    smem = pl.BlockSpec(memory_space=pltpu.SMEM)
    return [smem, smem,
            pl.BlockSpec((None, 4, B_QK_DIM), layer_map),
            pl.BlockSpec((None, 1, HEAD_DIM), layer_map)]


_K_COL_BLOCK = Q_WIDTH // KV_WIDTH
_V_COL_BLOCK = _K_COL_BLOCK + 1


def _ctx_attention(sel, qkv, a_sink, lam_init, b_lambda, b_subln, *, batch, seq):
    return pl.pallas_call(
        _ctx_attn_kernel,
        out_shape=jax.ShapeDtypeStruct((qkv.shape[0], Q_WIDTH), BF16),
        grid_spec=pltpu.PrefetchScalarGridSpec(
            num_scalar_prefetch=1,
            grid=(batch,),
            in_specs=[pl.BlockSpec((seq, Q_WIDTH), lambda b, s: (b, 0)),
                      pl.BlockSpec((seq, KV_WIDTH), lambda b, s: (b, _K_COL_BLOCK)),
                      pl.BlockSpec((seq, KV_WIDTH), lambda b, s: (b, _V_COL_BLOCK))] + _attn_param_specs(1),
            out_specs=pl.BlockSpec((seq, Q_WIDTH), lambda b, s: (b, 0)),
            scratch_shapes=[pltpu.VMEM((seq, KV_HEADS * VX), BF16)]),
        compiler_params=_params(("arbitrary",), 40 * 1024 * 1024),
        name="context_attention",
    )(sel, qkv, qkv, qkv, a_sink, lam_init, b_lambda, b_subln)


def _lat_attn_kernel(s_ref, o_in_ref, q_ref, k_ref, v_ref, cak_ref, cav_ref, cbk_ref, cbv_ref, cck_ref, ccv_ref,
                     sink_ref, lam_init_ref, bl_ref, subln_ref, o_ref, ck_ref, cvx_ref, vx_ref, *, lat_seq):
    del o_in_ref
    @pl.when(pl.program_id(1) == 0)
    def _():
        h0 = 0
        for kref, vref in ((cak_ref, cav_ref), (cbk_ref, cbv_ref), (cck_ref, ccv_ref)):
            n_heads = kref.shape[-1] // HEAD_DIM
            ck_ref[:, h0 * HEAD_DIM:(h0 + n_heads) * HEAD_DIM] = kref[...].astype(BF16)
            _store_values_with_ones(vref[...].astype(BF16), cvx_ref, n_heads, h0)
            h0 += n_heads
        _store_values_with_ones(v_ref[...], vx_ref, KV_HEADS)

    tq = q_ref.shape[0]
    band = tq + 2 * WINDOW
    q0 = pl.program_id(1) * tq
    k0 = pl.multiple_of(jnp.clip(q0 - WINDOW, 0, lat_seq - band), WINDOW)
    qpos = q0 + lax.broadcasted_iota(jnp.int32, (tq, 1), 0)
    kpos = k0 + lax.broadcasted_iota(jnp.int32, (1, band), 1)
    local_mask = jnp.abs(qpos - kpos) <= WINDOW

    every = slice(None)
    lat_rows = {"a": pl.ds(k0, band), "b": every, "c": every}
    segs = {g: [(_key_loader(ck_ref, every, h0), _value_loader(cvx_ref, every, h0)),
                (_key_loader(k_ref, lat_rows[g], h0), _value_loader(vx_ref, lat_rows[g], h0))]
            for g, h0 in _KV_HEAD0.items()}
    _attend(q_ref, segs, o_ref, sink_ref, lam_init_ref, bl_ref, subln_ref, s_ref[0], local_mask=local_mask)


def _lat_attention(sel, o, qkv, caches, a_sink, lam_init, b_lambda, b_subln, *, n_ctx_rows, lat_batch, lat_seq):
    q_blk0 = n_ctx_rows // TQ_LAT
    kv_blk0 = n_ctx_rows // lat_seq
    n_q = lat_seq // TQ_LAT
    past = caches[0].shape[2]

    def cache_spec(a):
        return pl.BlockSpec((None, None) + a.shape[2:], lambda b, i, s: (b, s[0], 0, 0))

    return pl.pallas_call(
        functools.partial(_lat_attn_kernel, lat_seq=lat_seq),
        out_shape=jax.ShapeDtypeStruct(o.shape, o.dtype),
        grid_spec=pltpu.PrefetchScalarGridSpec(
            num_scalar_prefetch=1,
            grid=(lat_batch, n_q),
            in_specs=[pl.BlockSpec(memory_space=pl.ANY),
                      pl.BlockSpec((TQ_LAT, Q_WIDTH), lambda b, i, s: (q_blk0 + b * n_q + i, 0)),
                      pl.BlockSpec((lat_seq, KV_WIDTH), lambda b, i, s: (kv_blk0 + b, _K_COL_BLOCK)),
                      pl.BlockSpec((lat_seq, KV_WIDTH), lambda b, i, s: (kv_blk0 + b, _V_COL_BLOCK))]
                     + [cache_spec(a) for a in caches] + _attn_param_specs(2),
            out_specs=pl.BlockSpec((TQ_LAT, Q_WIDTH), lambda b, i, s: (q_blk0 + b * n_q + i, 0)),
            scratch_shapes=[pltpu.VMEM((past, KV_WIDTH), BF16),
                            pltpu.VMEM((past, KV_HEADS * VX), BF16),
                            pltpu.VMEM((lat_seq, KV_HEADS * VX), BF16)]),
        input_output_aliases={1: 0},
        compiler_params=_params(("arbitrary", "arbitrary"), V7X_VMEM_LIMIT_BYTES),
        name="latent_attention",
    )(sel, o, qkv, qkv, qkv, *caches, a_sink, lam_init, b_lambda, b_subln)


def _out_proj_kernel(s_ref, a_ref, x_ref, mod_ref, gpost_ref, w_ref, o_ref, rs_ref):
    n = pl.program_id(1)
    y = _dot(a_ref[...], w_ref[...].astype(BF16))
    for slab in range(o_ref.shape[1] // TN_OUT):
        @pl.when(n == slab)
        def _():
            o_ref[:, slab * TN_OUT:(slab + 1) * TN_OUT] = y

    @pl.when(n == pl.num_programs(1) - 1)
    def _():
        _gated_residual_rows(x_ref, o_ref, o_ref, mod_ref, gpost_ref, rs_ref, 1.0)


def _out_proj(sel, a, x, mod, norm_post, w_out, *, n_ctx_rows, lat_rows):
    m, d = x.shape
    kdim = a.shape[1]
    cond = functools.partial(_cond_of_tile, tm=TM, n_ctx_rows=n_ctx_rows, lat_rows=lat_rows)
    mod_spec, gain_spec = _sublayer_specs(d, cond, 2)
    return pl.pallas_call(
        _out_proj_kernel,
        out_shape=jax.ShapeDtypeStruct((m, d), F32),
        grid_spec=pltpu.PrefetchScalarGridSpec(
            num_scalar_prefetch=1,
            grid=(m // TM, d // TN_OUT),
            in_specs=[
                pl.BlockSpec((TM, kdim), lambda i, n, s: (i, 0)),
                pl.BlockSpec((TM, d), lambda i, n, s: (i, 0)),
                mod_spec, gain_spec,
                pl.BlockSpec((None, kdim, TN_OUT), lambda i, n, s: (s[0], 0, n)),
            ],
            out_specs=pl.BlockSpec((TM, d), lambda i, n, s: (i, 0)),
            scratch_shapes=[pltpu.VMEM((TM, LANES), F32)]),
        compiler_params=_params(("arbitrary", "arbitrary"), V7X_VMEM_LIMIT_BYTES),
        name="out_projection",
    )(sel, a, x, mod, norm_post, w_out)


def _rope_tables(n_ctx_rows, lat_batch, lat_seq):
    t = np.arange(lat_seq)
    row = (t // GRID_W).astype(np.float32)
    col = (t % GRID_W).astype(np.float32)

    def tables(dim):
        nf = dim // 4
        inv = jnp.asarray(ROPE_THETA, F32) ** (-jnp.arange(nf, dtype=F32) / nf)
        ang = jnp.concatenate([row[:, None] * inv, col[:, None] * inv], axis=-1)
        cos, sin = jnp.cos(ang), jnp.sin(ang)
        cos_t = jnp.tile(jnp.concatenate([cos, cos], axis=-1), (1, LANES // dim))
        sin_t = jnp.tile(jnp.concatenate([-sin, sin], axis=-1), (1, LANES // dim))
        return (jnp.concatenate([jnp.ones((n_ctx_rows, LANES), F32)] + [cos_t] * lat_batch, axis=0),
                jnp.concatenate([jnp.zeros((n_ctx_rows, LANES), F32)] + [sin_t] * lat_batch, axis=0))

    ca, sa = tables(HEAD_DIM)
    cb, sb = tables(B_QK_DIM)
    return ca, sa, cb, sb


def kernel(x_prompt, x_sample, cache_a_k, cache_a_v, cache_b_k, cache_b_v, cache_c_k, cache_c_v,
           c, c_ctx, w_mod, b_mod, norm_pre, norm_post, ffn_gate, ffn_up, ffn_down,
           w_in, w_out, a_sink, b_lambda, b_subln, c_qnorm, c_knorm):
    batch, seq, d = x_prompt.shape
    lat_batch, lat_seq, _ = x_sample.shape
    depth = w_mod.shape[0]
    past = cache_a_k.shape[2]
    n_ctx_rows = batch * seq
    assert d == D_MODEL and depth == DEPTH and w_in.shape[-1] == IN_WIDTH
    assert n_ctx_rows % TM == 0 and lat_seq % TM == 0 and 1 + lat_batch <= N_COND
    assert n_ctx_rows % lat_seq == 0 and lat_seq % TQ_LAT == 0

    x = jnp.concatenate([x_prompt.reshape(n_ctx_rows, d), x_sample.reshape(lat_batch * lat_seq, d)], axis=0)
    cond = jnp.concatenate([c_ctx[None, :], c, jnp.zeros((N_COND - 1 - lat_batch, d), F32)], axis=0)
    mod = _modulation(cond, w_mod, b_mod).reshape(depth, N_COND, N_SUB, 3, d)
    gpre = norm_pre.reshape(depth, N_SUB, 1, d)
    gpost = norm_post.reshape(depth, N_SUB, 1, d)
    tables = _rope_tables(n_ctx_rows, lat_batch, lat_seq)
    caches = [a.reshape(lat_batch, depth, past, -1) for a in
              (cache_a_k, cache_a_v, cache_b_k, cache_b_v, cache_c_k, cache_c_v)]
    lam_init = jnp.asarray([0.8 - 0.6 * math.exp(-0.3 * l) for l in range(depth)], F32)
    subln = b_subln.reshape(depth, 1, HEAD_DIM)
    qnorm = c_qnorm.reshape(depth, 1, HEAD_DIM)
    knorm = c_knorm.reshape(depth, 1, HEAD_DIM)
    geom = dict(n_ctx_rows=n_ctx_rows, lat_rows=lat_seq)

    new_kv = [[] for _ in range(6)]
    for l in range(depth):
        sel = lambda sub, which=0: jnp.asarray([l, sub, which], jnp.int32)
        x = _ffn(sel(0, 0), x, mod, gpre, gpost, ffn_gate, ffn_up, ffn_down, **geom)
        qkv, kvc = _in_proj(sel(1), x, mod, gpre, w_in, tables, qnorm, knorm, **geom)
        o = _ctx_attention(sel(1), qkv, a_sink, lam_init, b_lambda, subln, batch=batch, seq=seq)
        o = _lat_attention(sel(1), o, qkv, caches, a_sink, lam_init, b_lambda, subln,
                           n_ctx_rows=n_ctx_rows, lat_batch=lat_batch, lat_seq=lat_seq)
        x = _out_proj(sel(1), o, x, mod, gpost, w_out, **geom)
        x = _ffn(sel(2, 1), x, mod, gpre, gpost, ffn_gate, ffn_up, ffn_down, **geom)

        c0 = 0
        for lst, heads in zip(new_kv, (A_KV, A_KV, B_HEADS, B_HEADS, C_KV, C_KV)):
            lst.append(kvc[:, c0:c0 + heads * HEAD_DIM].reshape(batch, seq, heads, HEAD_DIM))
            c0 += heads * HEAD_DIM

    y_prompt = x[:n_ctx_rows].reshape(batch, seq, d)
    y_sample = x[n_ctx_rows:].reshape(lat_batch, lat_seq, d)
    return (y_prompt, y_sample) + tuple(jnp.stack(t, axis=1) for t in new_kv)
```

```python
import functools
import math

import jax
import jax.numpy as jnp
import numpy as np
from jax import lax
from jax.experimental import pallas as pl
from jax.experimental.pallas import tpu as pltpu

D_MODEL = 2048
DEPTH = 4
GRID_W = 64
HEAD_DIM = 128
A_HEADS, A_KV = 6, 2
B_HEADS, B_QK_DIM = 4, 64
C_HEADS, C_KV = 6, 2
D_FF = 5632
WINDOW = 128
ROPE_THETA = 10000.0
EPS = 1e-6
FFN_RESID = 0.5
N_SUB = 3
N_COND = 8
Q_WIDTH = (A_HEADS + B_HEADS + C_HEADS) * HEAD_DIM
KV_HEADS = A_KV + B_HEADS + C_KV
KV_WIDTH = KV_HEADS * HEAD_DIM
IN_WIDTH = Q_WIDTH + 2 * KV_WIDTH
LOG2_E = math.log2(math.e)
_OFF = {}
_o = 0
for _name, _w in (("qa", A_HEADS), ("ka", A_KV), ("va", A_KV), ("qb", B_HEADS), ("kb", B_HEADS),
                  ("vb", B_HEADS), ("qc", C_HEADS), ("kc", C_KV), ("vc", C_KV)):
    _OFF[_name] = _o
    _o += _w * HEAD_DIM

V7X_VMEM_LIMIT_BYTES = 56 * 1024 * 1024
LANES = 128

TM = 1024
TF = 256
TN_IN = 256
TN_OUT = 1024
TN_MOD = 1024
ROW_CHUNK = 32
TQ_LAT = 256

F32 = jnp.float32
BF16 = jnp.bfloat16


def _rms(x):
    return x * lax.rsqrt(jnp.mean(x * x, axis=-1, keepdims=True) + EPS)


def _dot(a, b):
    return jnp.dot(a, b, preferred_element_type=F32)


def _dot_t(a, b):
    return lax.dot_general(a, b, (((1,), (1,)), ((), ())), preferred_element_type=F32)


def _cond_of_tile(i, tm, n_ctx_rows, lat_rows):
    return jnp.maximum((i * tm - n_ctx_rows) // lat_rows + 1, 0)


def _params(semantics, vmem_bytes):
    return pltpu.CompilerParams(dimension_semantics=semantics, vmem_limit_bytes=vmem_bytes)


def _mod_kernel(cond_ref, w_ref, b_ref, o_ref):
    c = cond_ref[...]
    s = (c * jax.nn.sigmoid(c)).astype(BF16)
    o_ref[...] = _dot(s, w_ref[...].astype(BF16)) + b_ref[...]


def _modulation(cond, w_mod, b_mod):
    depth, d, n = w_mod.shape
    return pl.pallas_call(
        _mod_kernel,
        out_shape=jax.ShapeDtypeStruct((depth, N_COND, n), F32),
        grid=(depth, n // TN_MOD),
        in_specs=[
            pl.BlockSpec((N_COND, d), lambda l, j: (0, 0)),
            pl.BlockSpec((None, d, TN_MOD), lambda l, j: (l, 0, j)),
            pl.BlockSpec((None, 1, TN_MOD), lambda l, j: (l, 0, j)),
        ],
        out_specs=pl.BlockSpec((None, N_COND, TN_MOD), lambda l, j: (l, 0, j)),
        compiler_params=_params(("arbitrary", "arbitrary"), 40 * 1024 * 1024),
        name="adaln_modulation",
    )(cond, w_mod, b_mod.reshape(depth, 1, n))


def _row_chunks(n_rows, body, unroll):
    def step(r, carry):
        body(pl.ds(pl.multiple_of(r * ROW_CHUNK, ROW_CHUNK), ROW_CHUNK))
        return carry

    lax.fori_loop(0, n_rows // ROW_CHUNK, step, 0, unroll=unroll)


def _row_rsqrt_mean_square(src_ref, rs_ref):
    def body(rows):
        x = src_ref[rows, :]
        ms = jnp.mean(x * x, axis=-1, keepdims=True)
        rs_ref[rows, :] = jnp.broadcast_to(lax.rsqrt(ms + EPS), (ROW_CHUNK, LANES))

    _row_chunks(src_ref.shape[0], body, unroll=8)


def _modulated_norm_rows(x_ref, mod_ref, gpre_ref, h_ref, rs_ref):
    _row_rsqrt_mean_square(x_ref, rs_ref)
    shift = mod_ref[0:1, :]
    gain = gpre_ref[...] * (1.0 + mod_ref[1:2, :])

    def body(rows):
        rs = rs_ref[rows, :]
        for t in range(x_ref.shape[1] // LANES):
            cols = slice(t * LANES, (t + 1) * LANES)
            h_ref[rows, cols] = (x_ref[rows, cols] * rs * gain[:, cols] + shift[:, cols]).astype(BF16)

    _row_chunks(x_ref.shape[0], body, unroll=2)


def _gated_residual_rows(x_ref, y_ref, o_ref, mod_ref, gpost_ref, rs_ref, weight):
    _row_rsqrt_mean_square(y_ref, rs_ref)
    gain = (weight * mod_ref[2:3, :]) * gpost_ref[...]

    def body(rows):
        rs = rs_ref[rows, :]
        for t in range(x_ref.shape[1] // LANES):
            cols = slice(t * LANES, (t + 1) * LANES)
            o_ref[rows, cols] = x_ref[rows, cols] + y_ref[rows, cols] * rs * gain[:, cols]

    _row_chunks(x_ref.shape[0], body, unroll=2)


def _sublayer_specs(d, cond, n_grid):
    def mod_map(*args):
        i, s = args[0], args[n_grid]
        return (s[0], cond(i), s[1], 0, 0)

    def gain_map(*args):
        s = args[n_grid]
        return (s[0], s[1], 0, 0)

    return (pl.BlockSpec((None, None, None, 3, d), mod_map),
            pl.BlockSpec((None, None, 1, d), gain_map))


def _ffn_kernel(s_ref, x_ref, mod_ref, gpre_ref, gpost_ref, wg_ref, wu_ref, wd_ref, o_ref, h_ref, rs_ref):
    f = pl.program_id(1)

    @pl.when(f == 0)
    def _():
        _modulated_norm_rows(x_ref, mod_ref, gpre_ref, h_ref, rs_ref)
        o_ref[...] = jnp.zeros_like(o_ref)

    h = h_ref[...]
    g = _dot(h, wg_ref[...].astype(BF16))
    u = _dot(h, wu_ref[...].astype(BF16))
    a = (g * jax.nn.sigmoid(g) * u).astype(BF16)
    wd = wd_ref[...].astype(BF16)
    nchunk = 512
    for n in range(0, D_MODEL, nchunk):
        o_ref[:, n:n + nchunk] += _dot(a, wd[:, n:n + nchunk])

    @pl.when(f == pl.num_programs(1) - 1)
    def _():
        _gated_residual_rows(x_ref, o_ref, o_ref, mod_ref, gpost_ref, rs_ref, FFN_RESID)


def _ffn(sel, x, mod, norm_pre, norm_post, w_gate, w_up, w_down, *, n_ctx_rows, lat_rows):
    m, d = x.shape
    cond = functools.partial(_cond_of_tile, tm=TM, n_ctx_rows=n_ctx_rows, lat_rows=lat_rows)
    mod_spec, gain_spec = _sublayer_specs(d, cond, 2)
    return pl.pallas_call(
        _ffn_kernel,
        out_shape=jax.ShapeDtypeStruct((m, d), F32),
        grid_spec=pltpu.PrefetchScalarGridSpec(
            num_scalar_prefetch=1,
            grid=(m // TM, D_FF // TF),
            in_specs=[
                pl.BlockSpec((TM, d), lambda i, f, s: (i, 0)),
                mod_spec, gain_spec, gain_spec,
                pl.BlockSpec((None, None, d, TF), lambda i, f, s: (s[0], s[2], 0, f)),
                pl.BlockSpec((None, None, d, TF), lambda i, f, s: (s[0], s[2], 0, f)),
                pl.BlockSpec((None, None, TF, d), lambda i, f, s: (s[0], s[2], f, 0)),
            ],
            out_specs=pl.BlockSpec((TM, d), lambda i, f, s: (i, 0)),
            scratch_shapes=[pltpu.VMEM((TM, d), BF16), pltpu.VMEM((TM, LANES), F32)]),
        compiler_params=_params(("arbitrary", "arbitrary"), V7X_VMEM_LIMIT_BYTES),
        name="ffn_sublayer",
    )(sel, x, mod, norm_pre, norm_post, w_gate, w_up, w_down)


_ROPE_NONE, _ROPE_HD, _ROPE_QK = 0, 1, 2
_NORM_NONE, _NORM_Q, _NORM_K = 0, 1, 2
_SCALE_ONE, _SCALE_HD, _SCALE_QK = 0, 1, 2
N_KV_STEPS = 2 * KV_WIDTH // TN_IN


def _qkv_steps():
    blk = lambda name, j: _OFF[name] // TN_IN + j
    q0, k0, v0 = 0, Q_WIDTH // TN_IN, (Q_WIDTH + KV_WIDTH) // TN_IN
    kv = [(blk("ka", 0), k0 + 0, _ROPE_HD, _NORM_NONE), (blk("va", 0), v0 + 0, _ROPE_NONE, _NORM_NONE),
          (blk("kb", 0), k0 + 1, _ROPE_QK, _NORM_NONE), (blk("kb", 1), k0 + 2, _ROPE_QK, _NORM_NONE),
          (blk("vb", 0), v0 + 1, _ROPE_NONE, _NORM_NONE), (blk("vb", 1), v0 + 2, _ROPE_NONE, _NORM_NONE),
          (blk("kc", 0), k0 + 3, _ROPE_HD, _NORM_K), (blk("vc", 0), v0 + 3, _ROPE_NONE, _NORM_NONE)]
    rows = [(src, dst, j, rope, norm, _SCALE_ONE) for j, (src, dst, rope, norm) in enumerate(kv)]
    last = len(kv) - 1
    qs = ([(blk("qa", j), q0 + j, _ROPE_HD, _NORM_NONE, _SCALE_HD) for j in range(3)]
          + [(blk("qb", j), q0 + 3 + j, _ROPE_QK, _NORM_NONE, _SCALE_QK) for j in range(2)]
          + [(blk("qc", j), q0 + 5 + j, _ROPE_HD, _NORM_Q, _SCALE_HD) for j in range(3)])
    rows += [(src, dst, last, rope, norm, scale) for src, dst, rope, norm, scale in qs]
    return np.asarray(rows, np.int32)


_QKV_STEPS = _qkv_steps()
_ST_SRC, _ST_DST, _ST_KVC, _ST_ROPE, _ST_NORM, _ST_SCALE = range(6)
N_QKV_STEPS = len(_QKV_STEPS)
QKV_SUB_ROWS = 256


def _in_proj_kernel(s_ref, st_ref, *refs, n_ctx_tiles, has_prev_cache):
    if has_prev_cache:
        refs = refs[1:]
    (x_ref, mod_ref, gpre_ref, w_ref, ca_ref, sa_ref, cb_ref, sb_ref, qn_ref, kn_ref,
     qkv_ref, kvc_ref, h_ref, rs_ref, n_ref, pa_ref, pb_ref) = refs
    i, c = pl.program_id(0), pl.program_id(1)

    @pl.when(c == 0)
    def _():
        _modulated_norm_rows(x_ref, mod_ref, gpre_ref, h_ref, rs_ref)

    def project(p_ref):
        p_ref[...] = _dot(h_ref[...], w_ref[...])

    b = jnp.maximum(c - 1, 0)
    rope, norm, scale_kind = st_ref[b, _ST_ROPE], st_ref[b, _ST_NORM], st_ref[b, _ST_SCALE]

    def finish(p_ref, rope_kind, use_norm):
        gain = jnp.where(norm == _NORM_Q, qn_ref[...], kn_ref[...])
        scale = jnp.where(scale_kind == _SCALE_HD, HEAD_DIM ** -0.5 * LOG2_E,
                          jnp.where(scale_kind == _SCALE_QK, B_QK_DIM ** -0.5 * LOG2_E, 1.0))
        lane = lax.broadcasted_iota(jnp.int32, (1, LANES), 1)
        low_half = (lane % B_QK_DIM) < (B_QK_DIM // 2)
        cos_ref, sin_ref = {_ROPE_NONE: (None, None), _ROPE_HD: (ca_ref, sa_ref),
                            _ROPE_QK: (cb_ref, sb_ref)}[rope_kind]
        for r in range(p_ref.shape[0] // QKV_SUB_ROWS):
            rows = slice(r * QKV_SUB_ROWS, (r + 1) * QKV_SUB_ROWS)
            for hh in range(TN_IN // HEAD_DIM):
                cols = slice(hh * HEAD_DIM, (hh + 1) * HEAD_DIM)
                n = p_ref[rows, cols]
                if use_norm:
                    n = _rms(n) * gain
                n_ref[rows, cols] = n
                if rope_kind == _ROPE_HD:
                    partner = pltpu.roll(n, HEAD_DIM // 2, axis=1)
                elif rope_kind == _ROPE_QK:
                    partner = jnp.where(low_half, pltpu.roll(n, LANES - B_QK_DIM // 2, axis=1),
                                        pltpu.roll(n, B_QK_DIM // 2, axis=1))
                if rope_kind != _ROPE_NONE:
                    n = n * cos_ref[rows, :] + partner * sin_ref[rows, :]
                qkv_ref[rows, cols] = (n * scale).astype(BF16)

        @pl.when((b < N_KV_STEPS) & (i < n_ctx_tiles))
        def _():
            seq = kvc_ref.shape[1]
            for bb in range(kvc_ref.shape[0]):
                kvc_ref[bb] = n_ref[bb * seq:(bb + 1) * seq, :]

    recipes = sorted({(int(r[_ST_ROPE]), int(r[_ST_NORM]) != _NORM_NONE) for r in _QKV_STEPS})

    def finish_by_recipe(when, project_ref, finish_ref):
        for rope_kind, use_norm in recipes:
            @pl.when(when & (rope == rope_kind) & ((norm != _NORM_NONE) == use_norm))
            def _():
                if project_ref is not None:
                    project(project_ref)
                finish(finish_ref, rope_kind, use_norm)

    odd = c % 2 == 1
    inner = (c > 0) & (c < N_QKV_STEPS)

    @pl.when(c == 0)
    def _():
        project(pa_ref)

    finish_by_recipe(inner & odd, pb_ref, pa_ref)
    finish_by_recipe(inner & jnp.logical_not(odd), pa_ref, pb_ref)
    finish_by_recipe(c == N_QKV_STEPS, None, pa_ref if N_QKV_STEPS % 2 == 1 else pb_ref)


def _in_proj(sel, x, mod, norm_pre, w_in, tables, c_qnorm, c_knorm, prev_cache, *,
             n_ctx_rows, lat_rows, batch, seq):
    m, d = x.shape
    depth = w_in.shape[0]
    n_ctx_tiles = n_ctx_rows // TM
    tile_batches = TM // seq
    cond = functools.partial(_cond_of_tile, tm=TM, n_ctx_rows=n_ctx_rows, lat_rows=lat_rows)
    mod_spec, gain_spec = _sublayer_specs(d, cond, 2)
    table = pl.BlockSpec((TM, LANES), lambda i, c, s, st: (i, 0))
    head_gain = pl.BlockSpec((None, 1, HEAD_DIM), lambda i, c, s, st: (s[0], 0, 0))
    last_kvc = N_KV_STEPS - 1
    projected = lambda c: jnp.minimum(c, N_QKV_STEPS - 1)
    finished = lambda c: jnp.maximum(c - 1, 0)

    def kvc_map(i, c, s, st):
        ctx = i < n_ctx_tiles
        return (jnp.minimum(i, n_ctx_tiles - 1), s[0], 0, jnp.where(ctx, st[finished(c), _ST_KVC], last_kvc))

    has_prev = prev_cache is not None
    prev_specs = [pl.BlockSpec(memory_space=pl.ANY)] if has_prev else []
    prev_args = [prev_cache] if has_prev else []
    p_buffer = pltpu.VMEM((TM, TN_IN), F32)
    return pl.pallas_call(
        functools.partial(_in_proj_kernel, n_ctx_tiles=n_ctx_tiles, has_prev_cache=has_prev),
        out_shape=(jax.ShapeDtypeStruct((m, IN_WIDTH), BF16),
                   jax.ShapeDtypeStruct((batch, depth, seq, N_KV_STEPS * TN_IN), F32)),
        grid_spec=pltpu.PrefetchScalarGridSpec(
            num_scalar_prefetch=2,
            grid=(m // TM, N_QKV_STEPS + 1),
            in_specs=prev_specs + [
                pl.BlockSpec((TM, d), lambda i, c, s, st: (i, 0)),
                mod_spec, gain_spec,
                pl.BlockSpec((None, d, TN_IN), lambda i, c, s, st: (s[0], 0, st[projected(c), _ST_SRC])),
                table, table, table, table, head_gain, head_gain,
            ],
            out_specs=(pl.BlockSpec((TM, TN_IN), lambda i, c, s, st: (i, st[finished(c), _ST_DST])),
                       pl.BlockSpec((tile_batches, None, seq, TN_IN), kvc_map)),
            scratch_shapes=[pltpu.VMEM((TM, d), BF16), pltpu.VMEM((TM, LANES), F32),
                            pltpu.VMEM((TM, TN_IN), F32), p_buffer, p_buffer]),
        input_output_aliases={2: 1} if has_prev else {},
        compiler_params=_params(("arbitrary", "arbitrary"), 48 * 1024 * 1024),
        name="qkv_projection",
    )(sel, jnp.asarray(_QKV_STEPS), *prev_args, x, mod, norm_pre, w_in, *tables, c_qnorm, c_knorm)


VX = 2 * HEAD_DIM


def _store_values_with_ones(v, vx_ref, n_heads, head0=0):
    ones = jnp.ones((v.shape[0], HEAD_DIM), BF16)
    for h in range(n_heads):
        c0 = (head0 + h) * VX
        vx_ref[:, c0:c0 + HEAD_DIM] = v[:, h * HEAD_DIM:(h + 1) * HEAD_DIM]
        vx_ref[:, c0 + HEAD_DIM:c0 + VX] = ones


def _key_loader(ref, rows, head0):
    return lambda h: ref[rows, (head0 + h) * HEAD_DIM:(head0 + h + 1) * HEAD_DIM]


def _value_loader(ref, rows, head0):
    return lambda h: ref[rows, (head0 + h) * VX:(head0 + h + 1) * VX]


_KV_HEAD0 = {"a": 0, "b": A_KV, "c": A_KV + B_HEADS}


def _exp_weighted(scores, values, floor=None):
    m = functools.reduce(jnp.maximum, [jnp.max(s, axis=-1, keepdims=True) for s in scores])
    if floor is not None:
        m = jnp.maximum(m, floor)
    acc = functools.reduce(lambda a, b: a + b,
                           [_dot(jnp.exp2(s - m).astype(BF16), v) for s, v in zip(scores, values)])
    return acc, m


def _attend(q_ref, segs, o_ref, sink_ref, lam_init_ref, bl_ref, subln_ref, layer, local_mask=None):
    lam_init = lam_init_ref[layer]
    bl = bl_ref[...]
    s1 = jnp.sum(bl[0:1, :] * bl[1:2, :], axis=-1, keepdims=True)
    s2 = jnp.sum(bl[2:3, :] * bl[3:4, :], axis=-1, keepdims=True)
    lam = jnp.exp(s1) - jnp.exp(s2) + lam_init
    g_sub = subln_ref[...] * (1.0 - lam_init)
    lane = lax.broadcasted_iota(jnp.int32, (1, HEAD_DIM), 1)
    first_map = lane < B_QK_DIM

    def split(acc):
        return acc[:, :HEAD_DIM], acc[:, HEAD_DIM:]

    for i in range(A_HEADS):
        kv = i // (A_HEADS // A_KV)
        q = q_ref[:, i * HEAD_DIM:(i + 1) * HEAD_DIM]
        scores = [_dot_t(q, k(kv)) for k, _ in segs["a"]]
        if local_mask is not None:
            scores[-1] = jnp.where(local_mask, scores[-1], -jnp.inf)
        sink = sink_ref[layer, i] * LOG2_E
        acc, m = _exp_weighted(scores, [v(kv) for _, v in segs["a"]], floor=sink)
        num, den = split(acc)
        o = num / (den + jnp.exp2(sink - m))
        o_ref[:, i * HEAD_DIM:(i + 1) * HEAD_DIM] = o.astype(o_ref.dtype)

    for i in range(B_HEADS):
        c0 = (A_HEADS + i) * HEAD_DIM
        q = q_ref[:, c0:c0 + HEAD_DIM]
        q1 = jnp.where(first_map, q, jnp.zeros_like(q))
        q2 = jnp.where(first_map, jnp.zeros_like(q), q)
        values = [v(i) for _, v in segs["b"]]
        n1, d1 = split(_exp_weighted([_dot_t(q1, k(i)) for k, _ in segs["b"]], values)[0])
        n2, d2 = split(_exp_weighted([_dot_t(q2, k(i)) for k, _ in segs["b"]], values)[0])
        o = n1 / d1 - lam * (n2 / d2)
        o = _rms(o) * g_sub
        o_ref[:, c0:c0 + HEAD_DIM] = o.astype(o_ref.dtype)

    for i in range(C_HEADS):
        kv = i // (C_HEADS // C_KV)
        c0 = (A_HEADS + B_HEADS + i) * HEAD_DIM
        q = q_ref[:, c0:c0 + HEAD_DIM]
        acc, _ = _exp_weighted([_dot_t(q, k(kv)) for k, _ in segs["c"]], [v(kv) for _, v in segs["c"]])
        num, den = split(acc)
        o_ref[:, c0:c0 + HEAD_DIM] = (num / den).astype(o_ref.dtype)


def _ctx_attn_kernel(s_ref, q_ref, k_ref, v_ref, sink_ref, lam_init_ref, bl_ref, subln_ref, o_ref, vx_ref):
    _store_values_with_ones(v_ref[...], vx_ref, KV_HEADS)
    rows = slice(None)
    segs = {g: [(_key_loader(k_ref, rows, h0), _value_loader(vx_ref, rows, h0))]
            for g, h0 in _KV_HEAD0.items()}
    _attend(q_ref, segs, o_ref, sink_ref, lam_init_ref, bl_ref, subln_ref, s_ref[0])


def _attn_param_specs(n_grid):
    def layer_map(*args):
        return (args[n_grid][0], 0, 0)

    smem = pl.BlockSpec(memory_space=pltpu.SMEM)
    return [smem, smem,
            pl.BlockSpec((None, 4, B_QK_DIM), layer_map),
            pl.BlockSpec((None, 1, HEAD_DIM), layer_map)]


_K_COL_BLOCK = Q_WIDTH // KV_WIDTH
_V_COL_BLOCK = _K_COL_BLOCK + 1


def _ctx_attention(sel, qkv, a_sink, lam_init, b_lambda, b_subln, *, batch, seq):
    return pl.pallas_call(
        _ctx_attn_kernel,
        out_shape=jax.ShapeDtypeStruct((qkv.shape[0], Q_WIDTH), BF16),
        grid_spec=pltpu.PrefetchScalarGridSpec(
            num_scalar_prefetch=1,
            grid=(batch,),
            in_specs=[pl.BlockSpec((seq, Q_WIDTH), lambda b, s: (b, 0)),
                      pl.BlockSpec((seq, KV_WIDTH), lambda b, s: (b, _K_COL_BLOCK)),
                      pl.BlockSpec((seq, KV_WIDTH), lambda b, s: (b, _V_COL_BLOCK))] + _attn_param_specs(1),
            out_specs=pl.BlockSpec((seq, Q_WIDTH), lambda b, s: (b, 0)),
            scratch_shapes=[pltpu.VMEM((seq, KV_HEADS * VX), BF16)]),
        compiler_params=_params(("arbitrary",), 40 * 1024 * 1024),
        name="context_attention",
    )(sel, qkv, qkv, qkv, a_sink, lam_init, b_lambda, b_subln)


def _lat_attn_kernel(s_ref, o_in_ref, q_ref, k_ref, v_ref, cak_ref, cav_ref, cbk_ref, cbv_ref, cck_ref, ccv_ref,
                     sink_ref, lam_init_ref, bl_ref, subln_ref, o_ref, ck_ref, cvx_ref, vx_ref, *, lat_seq):
    del o_in_ref
    @pl.when(pl.program_id(1) == 0)
    def _():
        h0 = 0
        for kref, vref in ((cak_ref, cav_ref), (cbk_ref, cbv_ref), (cck_ref, ccv_ref)):
            n_heads = kref.shape[-1] // HEAD_DIM
            ck_ref[:, h0 * HEAD_DIM:(h0 + n_heads) * HEAD_DIM] = kref[...].astype(BF16)
            _store_values_with_ones(vref[...].astype(BF16), cvx_ref, n_heads, h0)
            h0 += n_heads
        _store_values_with_ones(v_ref[...], vx_ref, KV_HEADS)

    tq = q_ref.shape[0]
    band = tq + 2 * WINDOW
    q0 = pl.program_id(1) * tq
    k0 = pl.multiple_of(jnp.clip(q0 - WINDOW, 0, lat_seq - band), WINDOW)
    qpos = q0 + lax.broadcasted_iota(jnp.int32, (tq, 1), 0)
    kpos = k0 + lax.broadcasted_iota(jnp.int32, (1, band), 1)
    local_mask = jnp.abs(qpos - kpos) <= WINDOW

    every = slice(None)
    lat_rows = {"a": pl.ds(k0, band), "b": every, "c": every}
    segs = {g: [(_key_loader(ck_ref, every, h0), _value_loader(cvx_ref, every, h0)),
                (_key_loader(k_ref, lat_rows[g], h0), _value_loader(vx_ref, lat_rows[g], h0))]
            for g, h0 in _KV_HEAD0.items()}
    _attend(q_ref, segs, o_ref, sink_ref, lam_init_ref, bl_ref, subln_ref, s_ref[0], local_mask=local_mask)


def _lat_attention(sel, o, qkv, caches, a_sink, lam_init, b_lambda, b_subln, *, n_ctx_rows, lat_batch, lat_seq):
    q_blk0 = n_ctx_rows // TQ_LAT
    kv_blk0 = n_ctx_rows // lat_seq
    n_q = lat_seq // TQ_LAT
    past = caches[0].shape[2]

    def cache_spec(a):
        return pl.BlockSpec((None, None) + a.shape[2:], lambda b, i, s: (b, s[0], 0, 0))

    return pl.pallas_call(
        functools.partial(_lat_attn_kernel, lat_seq=lat_seq),
        out_shape=jax.ShapeDtypeStruct(o.shape, o.dtype),
        grid_spec=pltpu.PrefetchScalarGridSpec(
            num_scalar_prefetch=1,
            grid=(lat_batch, n_q),
            in_specs=[pl.BlockSpec(memory_space=pl.ANY),
                      pl.BlockSpec((TQ_LAT, Q_WIDTH), lambda b, i, s: (q_blk0 + b * n_q + i, 0)),
                      pl.BlockSpec((lat_seq, KV_WIDTH), lambda b, i, s: (kv_blk0 + b, _K_COL_BLOCK)),
                      pl.BlockSpec((lat_seq, KV_WIDTH), lambda b, i, s: (kv_blk0 + b, _V_COL_BLOCK))]
                     + [cache_spec(a) for a in caches] + _attn_param_specs(2),
            out_specs=pl.BlockSpec((TQ_LAT, Q_WIDTH), lambda b, i, s: (q_blk0 + b * n_q + i, 0)),
            scratch_shapes=[pltpu.VMEM((past, KV_WIDTH), BF16),
                            pltpu.VMEM((past, KV_HEADS * VX), BF16),
                            pltpu.VMEM((lat_seq, KV_HEADS * VX), BF16)]),
        input_output_aliases={1: 0},
        compiler_params=_params(("arbitrary", "arbitrary"), V7X_VMEM_LIMIT_BYTES),
        name="latent_attention",
    )(sel, o, qkv, qkv, qkv, *caches, a_sink, lam_init, b_lambda, b_subln)


def _out_proj_kernel(s_ref, a_ref, x_ref, mod_ref, gpost_ref, w_ref, o_ref, rs_ref):
    n = pl.program_id(1)
    y = _dot(a_ref[...], w_ref[...])
    for slab in range(o_ref.shape[1] // TN_OUT):
        @pl.when(n == slab)
        def _():
            o_ref[:, slab * TN_OUT:(slab + 1) * TN_OUT] = y

    @pl.when(n == pl.num_programs(1) - 1)
    def _():
        _gated_residual_rows(x_ref, o_ref, o_ref, mod_ref, gpost_ref, rs_ref, 1.0)


def _out_proj(sel, a, x, mod, norm_post, w_out, *, n_ctx_rows, lat_rows):
    m, d = x.shape
    kdim = a.shape[1]
    cond = functools.partial(_cond_of_tile, tm=TM, n_ctx_rows=n_ctx_rows, lat_rows=lat_rows)
    mod_spec, gain_spec = _sublayer_specs(d, cond, 2)
    return pl.pallas_call(
        _out_proj_kernel,
        out_shape=jax.ShapeDtypeStruct((m, d), F32),
        grid_spec=pltpu.PrefetchScalarGridSpec(
            num_scalar_prefetch=1,
            grid=(m // TM, d // TN_OUT),
            in_specs=[
                pl.BlockSpec((TM, kdim), lambda i, n, s: (i, 0)),
                pl.BlockSpec((TM, d), lambda i, n, s: (i, 0)),
                mod_spec, gain_spec,
                pl.BlockSpec((None, kdim, TN_OUT), lambda i, n, s: (s[0], 0, n)),
            ],
            out_specs=pl.BlockSpec((TM, d), lambda i, n, s: (i, 0)),
            scratch_shapes=[pltpu.VMEM((TM, LANES), F32)]),
        compiler_params=_params(("arbitrary", "arbitrary"), V7X_VMEM_LIMIT_BYTES),
        name="out_projection",
    )(sel, a, x, mod, norm_post, w_out)


def _rope_tables(n_ctx_rows, lat_batch, lat_seq):
    t = np.arange(lat_seq)
    row = (t // GRID_W).astype(np.float32)
    col = (t % GRID_W).astype(np.float32)

    def tables(dim):
        nf = dim // 4
        inv = jnp.asarray(ROPE_THETA, F32) ** (-jnp.arange(nf, dtype=F32) / nf)
        ang = jnp.concatenate([row[:, None] * inv, col[:, None] * inv], axis=-1)
        cos, sin = jnp.cos(ang), jnp.sin(ang)
        cos_t = jnp.tile(jnp.concatenate([cos, cos], axis=-1), (1, LANES // dim))
        sin_t = jnp.tile(jnp.concatenate([-sin, sin], axis=-1), (1, LANES // dim))
        return (jnp.concatenate([jnp.ones((n_ctx_rows, LANES), F32)] + [cos_t] * lat_batch, axis=0),
                jnp.concatenate([jnp.zeros((n_ctx_rows, LANES), F32)] + [sin_t] * lat_batch, axis=0))

    ca, sa = tables(HEAD_DIM)
    cb, sb = tables(B_QK_DIM)
    return ca, sa, cb, sb


def kernel(x_prompt, x_sample, cache_a_k, cache_a_v, cache_b_k, cache_b_v, cache_c_k, cache_c_v,
           c, c_ctx, w_mod, b_mod, norm_pre, norm_post, ffn_gate, ffn_up, ffn_down,
           w_in, w_out, a_sink, b_lambda, b_subln, c_qnorm, c_knorm):
    batch, seq, d = x_prompt.shape
    lat_batch, lat_seq, _ = x_sample.shape
    depth = w_mod.shape[0]
    past = cache_a_k.shape[2]
    n_ctx_rows = batch * seq
    assert d == D_MODEL and depth == DEPTH and w_in.shape[-1] == IN_WIDTH
    assert n_ctx_rows % TM == 0 and lat_seq % TM == 0 and 1 + lat_batch <= N_COND
    assert n_ctx_rows % lat_seq == 0 and lat_seq % TQ_LAT == 0 and TM % seq == 0

    x = jnp.concatenate([x_prompt.reshape(n_ctx_rows, d), x_sample.reshape(lat_batch * lat_seq, d)], axis=0)
    cond = jnp.concatenate([c_ctx[None, :], c, jnp.zeros((N_COND - 1 - lat_batch, d), F32)], axis=0)
    mod = _modulation(cond, w_mod, b_mod).reshape(depth, N_COND, N_SUB, 3, d)
    gpre = norm_pre.reshape(depth, N_SUB, 1, d)
    gpost = norm_post.reshape(depth, N_SUB, 1, d)
    tables = _rope_tables(n_ctx_rows, lat_batch, lat_seq)
    w_in = w_in.astype(BF16)
    w_out = w_out.astype(BF16)
    caches = [a.reshape(lat_batch, depth, past, -1) for a in
              (cache_a_k, cache_a_v, cache_b_k, cache_b_v, cache_c_k, cache_c_v)]
    lam_init = jnp.asarray([0.8 - 0.6 * math.exp(-0.3 * l) for l in range(depth)], F32)
    subln = b_subln.reshape(depth, 1, HEAD_DIM)
    qnorm = c_qnorm.reshape(depth, 1, HEAD_DIM)
    knorm = c_knorm.reshape(depth, 1, HEAD_DIM)
    geom = dict(n_ctx_rows=n_ctx_rows, lat_rows=lat_seq)

    kvc = None
    for l in range(depth):
        sel = lambda sub, which=0: jnp.asarray([l, sub, which], jnp.int32)
        x = _ffn(sel(0, 0), x, mod, gpre, gpost, ffn_gate, ffn_up, ffn_down, **geom)
        qkv, kvc = _in_proj(sel(1), x, mod, gpre, w_in, tables, qnorm, knorm, kvc,
                            batch=batch, seq=seq, **geom)
        o = _ctx_attention(sel(1), qkv, a_sink, lam_init, b_lambda, subln, batch=batch, seq=seq)
        o = _lat_attention(sel(1), o, qkv, caches, a_sink, lam_init, b_lambda, subln,
                           n_ctx_rows=n_ctx_rows, lat_batch=lat_batch, lat_seq=lat_seq)
        x = _out_proj(sel(1), o, x, mod, gpost, w_out, **geom)
        x = _ffn(sel(2, 1), x, mod, gpre, gpost, ffn_gate, ffn_up, ffn_down, **geom)

    new_kv, c0 = [], 0
    for heads in (A_KV, A_KV, B_HEADS, B_HEADS, C_KV, C_KV):
        new_kv.append(kvc[..., c0:c0 + heads * HEAD_DIM].reshape(batch, depth, seq, heads, HEAD_DIM))
        c0 += heads * HEAD_DIM

    y_prompt = x[:n_ctx_rows].reshape(batch, seq, d)
    y_sample = x[n_ctx_rows:].reshape(lat_batch, lat_seq, d)
    return (y_prompt, y_sample) + tuple(new_kv)
```

```python
import functools
import math

import jax
import jax.numpy as jnp
import numpy as np
from jax import lax
from jax.experimental import pallas as pl
from jax.experimental.pallas import tpu as pltpu

D_MODEL = 2048
DEPTH = 4
GRID_W = 64
HEAD_DIM = 128
A_HEADS, A_KV = 6, 2
B_HEADS, B_QK_DIM = 4, 64
C_HEADS, C_KV = 6, 2
D_FF = 5632
WINDOW = 128
ROPE_THETA = 10000.0
EPS = 1e-6
FFN_RESID = 0.5
N_SUB = 3
N_COND = 8
Q_WIDTH = (A_HEADS + B_HEADS + C_HEADS) * HEAD_DIM
KV_HEADS = A_KV + B_HEADS + C_KV
KV_WIDTH = KV_HEADS * HEAD_DIM
IN_WIDTH = Q_WIDTH + 2 * KV_WIDTH
LOG2_E = math.log2(math.e)
_OFF = {}
_o = 0
for _name, _w in (("qa", A_HEADS), ("ka", A_KV), ("va", A_KV), ("qb", B_HEADS), ("kb", B_HEADS),
                  ("vb", B_HEADS), ("qc", C_HEADS), ("kc", C_KV), ("vc", C_KV)):
    _OFF[_name] = _o
    _o += _w * HEAD_DIM

V7X_VMEM_LIMIT_BYTES = 56 * 1024 * 1024
LANES = 128

TM = 1024
TF = 256
TN_IN = 256
TN_OUT = 1024
TN_MOD = 1024
ROW_CHUNK = 32
TQ_LAT = 256

F32 = jnp.float32
BF16 = jnp.bfloat16


def _rms(x):
    return x * lax.rsqrt(jnp.mean(x * x, axis=-1, keepdims=True) + EPS)


def _dot(a, b):
    return jnp.dot(a, b, preferred_element_type=F32)


def _dot_t(a, b):
    return lax.dot_general(a, b, (((1,), (1,)), ((), ())), preferred_element_type=F32)


def _cond_of_tile(i, tm, n_ctx_rows, lat_rows):
    return jnp.maximum((i * tm - n_ctx_rows) // lat_rows + 1, 0)


def _params(semantics, vmem_bytes):
    return pltpu.CompilerParams(dimension_semantics=semantics, vmem_limit_bytes=vmem_bytes)


def _mod_kernel(cond_ref, w_ref, b_ref, o_ref):
    c = cond_ref[...]
    s = (c * jax.nn.sigmoid(c)).astype(BF16)
    o_ref[...] = _dot(s, w_ref[...].astype(BF16)) + b_ref[...]


def _modulation(cond, w_mod, b_mod):
    depth, d, n = w_mod.shape
    return pl.pallas_call(
        _mod_kernel,
        out_shape=jax.ShapeDtypeStruct((depth, N_COND, n), F32),
        grid=(depth, n // TN_MOD),
        in_specs=[
            pl.BlockSpec((N_COND, d), lambda l, j: (0, 0)),
            pl.BlockSpec((None, d, TN_MOD), lambda l, j: (l, 0, j)),
            pl.BlockSpec((None, 1, TN_MOD), lambda l, j: (l, 0, j)),
        ],
        out_specs=pl.BlockSpec((None, N_COND, TN_MOD), lambda l, j: (l, 0, j)),
        compiler_params=_params(("arbitrary", "arbitrary"), 40 * 1024 * 1024),
        name="adaln_modulation",
    )(cond, w_mod, b_mod.reshape(depth, 1, n))


def _row_chunks(n_rows, body, unroll):
    def step(r, carry):
        body(pl.ds(pl.multiple_of(r * ROW_CHUNK, ROW_CHUNK), ROW_CHUNK))
        return carry

    lax.fori_loop(0, n_rows // ROW_CHUNK, step, 0, unroll=unroll)


def _row_rsqrt_mean_square(src_ref, rs_ref):
    def body(rows):
        x = src_ref[rows, :]
        ms = jnp.mean(x * x, axis=-1, keepdims=True)
        rs_ref[rows, :] = jnp.broadcast_to(lax.rsqrt(ms + EPS), (ROW_CHUNK, LANES))

    _row_chunks(src_ref.shape[0], body, unroll=8)


def _modulated_norm_rows(x_ref, mod_ref, gpre_ref, h_ref, rs_ref):
    _row_rsqrt_mean_square(x_ref, rs_ref)
    shift = mod_ref[0:1, :]
    gain = gpre_ref[...] * (1.0 + mod_ref[1:2, :])

    def body(rows):
        rs = rs_ref[rows, :]
        for t in range(x_ref.shape[1] // LANES):
            cols = slice(t * LANES, (t + 1) * LANES)
            h_ref[rows, cols] = (x_ref[rows, cols] * rs * gain[:, cols] + shift[:, cols]).astype(BF16)

    _row_chunks(x_ref.shape[0], body, unroll=2)


def _gated_residual_rows(x_ref, y_ref, o_ref, mod_ref, gpost_ref, rs_ref, weight):
    _row_rsqrt_mean_square(y_ref, rs_ref)
    gain = (weight * mod_ref[2:3, :]) * gpost_ref[...]

    def body(rows):
        rs = rs_ref[rows, :]
        for t in range(x_ref.shape[1] // LANES):
            cols = slice(t * LANES, (t + 1) * LANES)
            o_ref[rows, cols] = x_ref[rows, cols] + y_ref[rows, cols] * rs * gain[:, cols]

    _row_chunks(x_ref.shape[0], body, unroll=2)


def _sublayer_specs(d, cond, n_grid):
    def mod_map(*args):
        i, s = args[0], args[n_grid]
        return (s[0], cond(i), s[1], 0, 0)

    def gain_map(*args):
        s = args[n_grid]
        return (s[0], s[1], 0, 0)

    return (pl.BlockSpec((None, None, None, 3, d), mod_map),
            pl.BlockSpec((None, None, 1, d), gain_map))


def _ffn_kernel(s_ref, x_ref, mod_ref, gpre_ref, gpost_ref, wg_ref, wu_ref, wd_ref, o_ref, h_ref, rs_ref):
    f = pl.program_id(1)

    @pl.when(f == 0)
    def _():
        _modulated_norm_rows(x_ref, mod_ref, gpre_ref, h_ref, rs_ref)
        o_ref[...] = jnp.zeros_like(o_ref)

    h = h_ref[...]
    g = _dot(h, wg_ref[...].astype(BF16))
    u = _dot(h, wu_ref[...].astype(BF16))
    a = (g * jax.nn.sigmoid(g) * u).astype(BF16)
    wd = wd_ref[...].astype(BF16)
    nchunk = 512
    for n in range(0, D_MODEL, nchunk):
        o_ref[:, n:n + nchunk] += _dot(a, wd[:, n:n + nchunk])

    @pl.when(f == pl.num_programs(1) - 1)
    def _():
        _gated_residual_rows(x_ref, o_ref, o_ref, mod_ref, gpost_ref, rs_ref, FFN_RESID)


def _ffn(sel, x, mod, norm_pre, norm_post, w_gate, w_up, w_down, *, n_ctx_rows, lat_rows):
    m, d = x.shape
    cond = functools.partial(_cond_of_tile, tm=TM, n_ctx_rows=n_ctx_rows, lat_rows=lat_rows)
    mod_spec, gain_spec = _sublayer_specs(d, cond, 2)
    return pl.pallas_call(
        _ffn_kernel,
        out_shape=jax.ShapeDtypeStruct((m, d), F32),
        grid_spec=pltpu.PrefetchScalarGridSpec(
            num_scalar_prefetch=1,
            grid=(m // TM, D_FF // TF),
            in_specs=[
                pl.BlockSpec((TM, d), lambda i, f, s: (i, 0)),
                mod_spec, gain_spec, gain_spec,
                pl.BlockSpec((None, None, d, TF), lambda i, f, s: (s[0], s[2], 0, f)),
                pl.BlockSpec((None, None, d, TF), lambda i, f, s: (s[0], s[2], 0, f)),
                pl.BlockSpec((None, None, TF, d), lambda i, f, s: (s[0], s[2], f, 0)),
            ],
            out_specs=pl.BlockSpec((TM, d), lambda i, f, s: (i, 0)),
            scratch_shapes=[pltpu.VMEM((TM, d), BF16), pltpu.VMEM((TM, LANES), F32)]),
        compiler_params=_params(("arbitrary", "arbitrary"), V7X_VMEM_LIMIT_BYTES),
        name="ffn_sublayer",
    )(sel, x, mod, norm_pre, norm_post, w_gate, w_up, w_down)


_ROPE_NONE, _ROPE_HD, _ROPE_QK = 0, 1, 2
_NORM_NONE, _NORM_Q, _NORM_K = 0, 1, 2
_SCALE_ONE, _SCALE_HD, _SCALE_QK = 0, 1, 2


def _qkv_steps():
    blk = lambda name, j: _OFF[name] // TN_IN + j
    q0, k0, v0 = 0, Q_WIDTH // TN_IN, (Q_WIDTH + KV_WIDTH) // TN_IN
    kv = [(blk("ka", 0), k0 + 0, _ROPE_HD, _NORM_NONE), (blk("va", 0), v0 + 0, _ROPE_NONE, _NORM_NONE),
          (blk("kb", 0), k0 + 1, _ROPE_QK, _NORM_NONE), (blk("kb", 1), k0 + 2, _ROPE_QK, _NORM_NONE),
          (blk("vb", 0), v0 + 1, _ROPE_NONE, _NORM_NONE), (blk("vb", 1), v0 + 2, _ROPE_NONE, _NORM_NONE),
          (blk("kc", 0), k0 + 3, _ROPE_HD, _NORM_K), (blk("vc", 0), v0 + 3, _ROPE_NONE, _NORM_NONE)]
    rows = [(src, dst, rope, norm, _SCALE_ONE) for src, dst, rope, norm in kv]
    rows += ([(blk("qa", j), q0 + j, _ROPE_HD, _NORM_NONE, _SCALE_HD) for j in range(3)]
             + [(blk("qb", j), q0 + 3 + j, _ROPE_QK, _NORM_NONE, _SCALE_QK) for j in range(2)]
             + [(blk("qc", j), q0 + 5 + j, _ROPE_HD, _NORM_Q, _SCALE_HD) for j in range(3)])
    return np.asarray(rows, np.int32)


_QKV_STEPS = _qkv_steps()
_CACHE_HEADS = (A_KV, A_KV, B_HEADS, B_HEADS, C_KV, C_KV)
_KV_STEP_CACHE = ((0, 0), (1, 0), (2, 0), (2, 2), (3, 0), (3, 2), (4, 0), (5, 0))
_ST_SRC, _ST_DST, _ST_ROPE, _ST_NORM, _ST_SCALE = range(5)
N_QKV_STEPS = len(_QKV_STEPS)
QKV_SUB_ROWS = 256


def _in_proj_kernel(s_ref, st_ref, *refs, n_ctx_tiles, has_prev_cache):
    n_caches = len(_CACHE_HEADS)
    if has_prev_cache:
        refs = refs[n_caches:]
    (x_ref, mod_ref, gpre_ref, w_ref, ca_ref, sa_ref, cb_ref, sb_ref, qn_ref, kn_ref, qkv_ref) = refs[:11]
    cache_refs = refs[11:11 + n_caches]
    h_ref, rs_ref, n_ref, pa_ref, pb_ref = refs[11 + n_caches:]
    i, c = pl.program_id(0), pl.program_id(1)

    @pl.when(c == 0)
    def _():
        _modulated_norm_rows(x_ref, mod_ref, gpre_ref, h_ref, rs_ref)

    def project(p_ref):
        p_ref[...] = _dot(h_ref[...], w_ref[...])

    b = jnp.maximum(c - 1, 0)
    rope, norm, scale_kind = st_ref[b, _ST_ROPE], st_ref[b, _ST_NORM], st_ref[b, _ST_SCALE]

    def finish(p_ref, rope_kind, use_norm):
        gain = jnp.where(norm == _NORM_Q, qn_ref[...], kn_ref[...])
        scale = jnp.where(scale_kind == _SCALE_HD, HEAD_DIM ** -0.5 * LOG2_E,
                          jnp.where(scale_kind == _SCALE_QK, B_QK_DIM ** -0.5 * LOG2_E, 1.0))
        lane = lax.broadcasted_iota(jnp.int32, (1, LANES), 1)
        low_half = (lane % B_QK_DIM) < (B_QK_DIM // 2)
        cos_ref, sin_ref = {_ROPE_NONE: (None, None), _ROPE_HD: (ca_ref, sa_ref),
                            _ROPE_QK: (cb_ref, sb_ref)}[rope_kind]
        for r in range(p_ref.shape[0] // QKV_SUB_ROWS):
            rows = slice(r * QKV_SUB_ROWS, (r + 1) * QKV_SUB_ROWS)
            for hh in range(TN_IN // HEAD_DIM):
                cols = slice(hh * HEAD_DIM, (hh + 1) * HEAD_DIM)
                n = p_ref[rows, cols]
                if use_norm:
                    n = _rms(n) * gain
                n_ref[rows, cols] = n
                if rope_kind == _ROPE_HD:
                    partner = pltpu.roll(n, HEAD_DIM // 2, axis=1)
                elif rope_kind == _ROPE_QK:
                    partner = jnp.where(low_half, pltpu.roll(n, LANES - B_QK_DIM // 2, axis=1),
                                        pltpu.roll(n, B_QK_DIM // 2, axis=1))
                if rope_kind != _ROPE_NONE:
                    n = n * cos_ref[rows, :] + partner * sin_ref[rows, :]
                qkv_ref[rows, cols] = (n * scale).astype(BF16)

    recipes = sorted({(int(r[_ST_ROPE]), int(r[_ST_NORM]) != _NORM_NONE) for r in _QKV_STEPS})

    def finish_by_recipe(when, project_ref, finish_ref):
        for rope_kind, use_norm in recipes:
            @pl.when(when & (rope == rope_kind) & ((norm != _NORM_NONE) == use_norm))
            def _():
                if project_ref is not None:
                    project(project_ref)
                finish(finish_ref, rope_kind, use_norm)

    odd = c % 2 == 1
    inner = (c > 0) & (c < N_QKV_STEPS)

    @pl.when(c == 0)
    def _():
        project(pa_ref)

    finish_by_recipe(inner & odd, pb_ref, pa_ref)
    finish_by_recipe(inner & jnp.logical_not(odd), pa_ref, pb_ref)
    finish_by_recipe(c == N_QKV_STEPS, None, pa_ref if N_QKV_STEPS % 2 == 1 else pb_ref)

    for step, (cache, head0) in enumerate(_KV_STEP_CACHE):
        @pl.when((c == step + 1) & (i < n_ctx_tiles))
        def _():
            cache_ref, heads = cache_refs[cache], _CACHE_HEADS[cache]
            seq = cache_ref.shape[1] // heads
            for bb in range(cache_ref.shape[0]):
                for hh in range(TN_IN // HEAD_DIM):
                    cache_ref[bb, pl.ds(head0 + hh, seq, stride=heads), :] = (
                        n_ref[bb * seq:(bb + 1) * seq, hh * HEAD_DIM:(hh + 1) * HEAD_DIM])


def _in_proj(sel, x, mod, norm_pre, w_in, tables, c_qnorm, c_knorm, prev_cache, *,
             n_ctx_rows, lat_rows, batch, seq):
    m, d = x.shape
    depth = w_in.shape[0]
    n_ctx_tiles = n_ctx_rows // TM
    tile_batches = TM // seq
    cond = functools.partial(_cond_of_tile, tm=TM, n_ctx_rows=n_ctx_rows, lat_rows=lat_rows)
    mod_spec, gain_spec = _sublayer_specs(d, cond, 2)
    table = pl.BlockSpec((TM, LANES), lambda i, c, s, st: (i, 0))
    head_gain = pl.BlockSpec((None, 1, HEAD_DIM), lambda i, c, s, st: (s[0], 0, 0))
    projected = lambda c: jnp.minimum(c, N_QKV_STEPS - 1)
    finished = lambda c: jnp.maximum(c - 1, 0)

    def cache_spec(heads):
        return pl.BlockSpec((tile_batches, None, seq * heads, HEAD_DIM),
                            lambda i, c, s, st: (jnp.minimum(i, n_ctx_tiles - 1), s[0], 0, 0))

    has_prev = prev_cache is not None
    n_caches = len(_CACHE_HEADS)
    prev_specs = [pl.BlockSpec(memory_space=pl.ANY)] * n_caches if has_prev else []
    prev_args = list(prev_cache) if has_prev else []
    p_buffer = pltpu.VMEM((TM, TN_IN), F32)
    return pl.pallas_call(
        functools.partial(_in_proj_kernel, n_ctx_tiles=n_ctx_tiles, has_prev_cache=has_prev),
        out_shape=(jax.ShapeDtypeStruct((m, IN_WIDTH), BF16),)
                  + tuple(jax.ShapeDtypeStruct((batch, depth, seq * heads, HEAD_DIM), F32)
                          for heads in _CACHE_HEADS),
        grid_spec=pltpu.PrefetchScalarGridSpec(
            num_scalar_prefetch=2,
            grid=(m // TM, N_QKV_STEPS + 1),
            in_specs=prev_specs + [
                pl.BlockSpec((TM, d), lambda i, c, s, st: (i, 0)),
                mod_spec, gain_spec,
                pl.BlockSpec((None, d, TN_IN), lambda i, c, s, st: (s[0], 0, st[projected(c), _ST_SRC])),
                table, table, table, table, head_gain, head_gain,
            ],
            out_specs=(pl.BlockSpec((TM, TN_IN), lambda i, c, s, st: (i, st[finished(c), _ST_DST])),)
                      + tuple(cache_spec(heads) for heads in _CACHE_HEADS),
            scratch_shapes=[pltpu.VMEM((TM, d), BF16), pltpu.VMEM((TM, LANES), F32),
                            pltpu.VMEM((TM, TN_IN), F32), p_buffer, p_buffer]),
        input_output_aliases={2 + k: 1 + k for k in range(n_caches)} if has_prev else {},
        compiler_params=_params(("arbitrary", "arbitrary"), V7X_VMEM_LIMIT_BYTES),
        name="qkv_projection",
    )(sel, jnp.asarray(_QKV_STEPS), *prev_args, x, mod, norm_pre, w_in, *tables, c_qnorm, c_knorm)


VX = 2 * HEAD_DIM


def _store_values_with_ones(v, vx_ref, n_heads, head0=0):
    ones = jnp.ones((v.shape[0], HEAD_DIM), BF16)
    for h in range(n_heads):
        c0 = (head0 + h) * VX
        vx_ref[:, c0:c0 + HEAD_DIM] = v[:, h * HEAD_DIM:(h + 1) * HEAD_DIM]
        vx_ref[:, c0 + HEAD_DIM:c0 + VX] = ones


def _key_loader(ref, rows, head0):
    return lambda h: ref[rows, (head0 + h) * HEAD_DIM:(head0 + h + 1) * HEAD_DIM]


def _value_loader(ref, rows, head0):
    return lambda h: ref[rows, (head0 + h) * VX:(head0 + h + 1) * VX]


_KV_HEAD0 = {"a": 0, "b": A_KV, "c": A_KV + B_HEADS}


def _exp_weighted(scores, values, floor=None):
    m = functools.reduce(jnp.maximum, [jnp.max(s, axis=-1, keepdims=True) for s in scores])
    if floor is not None:
        m = jnp.maximum(m, floor)
    acc = functools.reduce(lambda a, b: a + b,
                           [_dot(jnp.exp2(s - m).astype(BF16), v) for s, v in zip(scores, values)])
    return acc, m


def _attend(q_ref, segs, o_ref, sink_ref, lam_init_ref, bl_ref, subln_ref, layer, local_mask=None):
    lam_init = lam_init_ref[layer]
    bl = bl_ref[...]
    s1 = jnp.sum(bl[0:1, :] * bl[1:2, :], axis=-1, keepdims=True)
    s2 = jnp.sum(bl[2:3, :] * bl[3:4, :], axis=-1, keepdims=True)
    lam = jnp.exp(s1) - jnp.exp(s2) + lam_init
    g_sub = subln_ref[...] * (1.0 - lam_init)
    lane = lax.broadcasted_iota(jnp.int32, (1, HEAD_DIM), 1)
    first_map = lane < B_QK_DIM

    def split(acc):
        return acc[:, :HEAD_DIM], acc[:, HEAD_DIM:]

    for i in range(A_HEADS):
        kv = i // (A_HEADS // A_KV)
        q = q_ref[:, i * HEAD_DIM:(i + 1) * HEAD_DIM]
        scores = [_dot_t(q, k(kv)) for k, _ in segs["a"]]
        if local_mask is not None:
            scores[-1] = jnp.where(local_mask, scores[-1], -jnp.inf)
        sink = sink_ref[layer, i] * LOG2_E
        acc, m = _exp_weighted(scores, [v(kv) for _, v in segs["a"]], floor=sink)
        num, den = split(acc)
        o = num / (den + jnp.exp2(sink - m))
        o_ref[:, i * HEAD_DIM:(i + 1) * HEAD_DIM] = o.astype(o_ref.dtype)

    for i in range(B_HEADS):
        c0 = (A_HEADS + i) * HEAD_DIM
        q = q_ref[:, c0:c0 + HEAD_DIM]
        q1 = jnp.where(first_map, q, jnp.zeros_like(q))
        q2 = jnp.where(first_map, jnp.zeros_like(q), q)
        values = [v(i) for _, v in segs["b"]]
        n1, d1 = split(_exp_weighted([_dot_t(q1, k(i)) for k, _ in segs["b"]], values)[0])
        n2, d2 = split(_exp_weighted([_dot_t(q2, k(i)) for k, _ in segs["b"]], values)[0])
        o = n1 / d1 - lam * (n2 / d2)
        o = _rms(o) * g_sub
        o_ref[:, c0:c0 + HEAD_DIM] = o.astype(o_ref.dtype)

    for i in range(C_HEADS):
        kv = i // (C_HEADS // C_KV)
        c0 = (A_HEADS + B_HEADS + i) * HEAD_DIM
        q = q_ref[:, c0:c0 + HEAD_DIM]
        acc, _ = _exp_weighted([_dot_t(q, k(kv)) for k, _ in segs["c"]], [v(kv) for _, v in segs["c"]])
        num, den = split(acc)
        o_ref[:, c0:c0 + HEAD_DIM] = (num / den).astype(o_ref.dtype)


def _ctx_attn_kernel(s_ref, q_ref, k_ref, v_ref, sink_ref, lam_init_ref, bl_ref, subln_ref, o_ref, vx_ref):
    _store_values_with_ones(v_ref[...], vx_ref, KV_HEADS)
    rows = slice(None)
    segs = {g: [(_key_loader(k_ref, rows, h0), _value_loader(vx_ref, rows, h0))]
            for g, h0 in _KV_HEAD0.items()}
    _attend(q_ref, segs, o_ref, sink_ref, lam_init_ref, bl_ref, subln_ref, s_ref[0])


def _attn_param_specs(n_grid):
    def layer_map(*args):
        return (args[n_grid][0], 0, 0)

    smem = pl.BlockSpec(memory_space=pltpu.SMEM)
    return [smem, smem,
            pl.BlockSpec((None, 4, B_QK_DIM), layer_map),
            pl.BlockSpec((None, 1, HEAD_DIM), layer_map)]


_K_COL_BLOCK = Q_WIDTH // KV_WIDTH
_V_COL_BLOCK = _K_COL_BLOCK + 1


def _ctx_attention(sel, qkv, a_sink, lam_init, b_lambda, b_subln, *, batch, seq):
    return pl.pallas_call(
        _ctx_attn_kernel,
        out_shape=jax.ShapeDtypeStruct((qkv.shape[0], Q_WIDTH), BF16),
        grid_spec=pltpu.PrefetchScalarGridSpec(
            num_scalar_prefetch=1,
            grid=(batch,),
            in_specs=[pl.BlockSpec((seq, Q_WIDTH), lambda b, s: (b, 0)),
                      pl.BlockSpec((seq, KV_WIDTH), lambda b, s: (b, _K_COL_BLOCK)),
                      pl.BlockSpec((seq, KV_WIDTH), lambda b, s: (b, _V_COL_BLOCK))] + _attn_param_specs(1),
            out_specs=pl.BlockSpec((seq, Q_WIDTH), lambda b, s: (b, 0)),
            scratch_shapes=[pltpu.VMEM((seq, KV_HEADS * VX), BF16)]),
        compiler_params=_params(("arbitrary",), 40 * 1024 * 1024),
        name="context_attention",
    )(sel, qkv, qkv, qkv, a_sink, lam_init, b_lambda, b_subln)


def _lat_attn_kernel(s_ref, o_in_ref, q_ref, k_ref, v_ref, cak_ref, cav_ref, cbk_ref, cbv_ref, cck_ref, ccv_ref,
                     sink_ref, lam_init_ref, bl_ref, subln_ref, o_ref, ck_ref, cvx_ref, vx_ref, *, lat_seq):
    del o_in_ref
    @pl.when(pl.program_id(1) == 0)
    def _():
        past = ck_ref.shape[0]
        h0 = 0
        for kref, vref in ((cak_ref, cav_ref), (cbk_ref, cbv_ref), (cck_ref, ccv_ref)):
            n_heads = kref.shape[0] // past
            for h in range(n_heads):
                head_rows = pl.ds(h, past, stride=n_heads)
                ck_ref[:, (h0 + h) * HEAD_DIM:(h0 + h + 1) * HEAD_DIM] = kref[head_rows, :].astype(BF16)
                _store_values_with_ones(vref[head_rows, :].astype(BF16), cvx_ref, 1, h0 + h)
            h0 += n_heads
        _store_values_with_ones(v_ref[...], vx_ref, KV_HEADS)

    tq = q_ref.shape[0]
    band = tq + 2 * WINDOW
    q0 = pl.program_id(1) * tq
    k0 = pl.multiple_of(jnp.clip(q0 - WINDOW, 0, lat_seq - band), WINDOW)
    qpos = q0 + lax.broadcasted_iota(jnp.int32, (tq, 1), 0)
    kpos = k0 + lax.broadcasted_iota(jnp.int32, (1, band), 1)
    local_mask = jnp.abs(qpos - kpos) <= WINDOW

    every = slice(None)
    lat_rows = {"a": pl.ds(k0, band), "b": every, "c": every}
    segs = {g: [(_key_loader(ck_ref, every, h0), _value_loader(cvx_ref, every, h0)),
                (_key_loader(k_ref, lat_rows[g], h0), _value_loader(vx_ref, lat_rows[g], h0))]
            for g, h0 in _KV_HEAD0.items()}
    _attend(q_ref, segs, o_ref, sink_ref, lam_init_ref, bl_ref, subln_ref, s_ref[0], local_mask=local_mask)


def _lat_attention(sel, o, qkv, caches, a_sink, lam_init, b_lambda, b_subln, *, n_ctx_rows, lat_batch, lat_seq):
    q_blk0 = n_ctx_rows // TQ_LAT
    kv_blk0 = n_ctx_rows // lat_seq
    n_q = lat_seq // TQ_LAT
    past = caches[0].shape[2] // _CACHE_HEADS[0]

    def cache_spec(a):
        return pl.BlockSpec((None, None) + a.shape[2:], lambda b, i, s: (b, s[0], 0, 0))

    return pl.pallas_call(
        functools.partial(_lat_attn_kernel, lat_seq=lat_seq),
        out_shape=jax.ShapeDtypeStruct(o.shape, o.dtype),
        grid_spec=pltpu.PrefetchScalarGridSpec(
            num_scalar_prefetch=1,
            grid=(lat_batch, n_q),
            in_specs=[pl.BlockSpec(memory_space=pl.ANY),
                      pl.BlockSpec((TQ_LAT, Q_WIDTH), lambda b, i, s: (q_blk0 + b * n_q + i, 0)),
                      pl.BlockSpec((lat_seq, KV_WIDTH), lambda b, i, s: (kv_blk0 + b, _K_COL_BLOCK)),
                      pl.BlockSpec((lat_seq, KV_WIDTH), lambda b, i, s: (kv_blk0 + b, _V_COL_BLOCK))]
                     + [cache_spec(a) for a in caches] + _attn_param_specs(2),
            out_specs=pl.BlockSpec((TQ_LAT, Q_WIDTH), lambda b, i, s: (q_blk0 + b * n_q + i, 0)),
            scratch_shapes=[pltpu.VMEM((past, KV_WIDTH), BF16),
                            pltpu.VMEM((past, KV_HEADS * VX), BF16),
                            pltpu.VMEM((lat_seq, KV_HEADS * VX), BF16)]),
        input_output_aliases={1: 0},
        compiler_params=_params(("arbitrary", "arbitrary"), V7X_VMEM_LIMIT_BYTES),
        name="latent_attention",
    )(sel, o, qkv, qkv, qkv, *caches, a_sink, lam_init, b_lambda, b_subln)


def _out_proj_kernel(s_ref, a_ref, x_ref, mod_ref, gpost_ref, w_ref, o_ref, rs_ref):
    n = pl.program_id(1)
    y = _dot(a_ref[...], w_ref[...])
    for slab in range(o_ref.shape[1] // TN_OUT):
        @pl.when(n == slab)
        def _():
            o_ref[:, slab * TN_OUT:(slab + 1) * TN_OUT] = y

    @pl.when(n == pl.num_programs(1) - 1)
    def _():
        _gated_residual_rows(x_ref, o_ref, o_ref, mod_ref, gpost_ref, rs_ref, 1.0)


def _out_proj(sel, a, x, mod, norm_post, w_out, *, n_ctx_rows, lat_rows):
    m, d = x.shape
    kdim = a.shape[1]
    cond = functools.partial(_cond_of_tile, tm=TM, n_ctx_rows=n_ctx_rows, lat_rows=lat_rows)
    mod_spec, gain_spec = _sublayer_specs(d, cond, 2)
    return pl.pallas_call(
        _out_proj_kernel,
        out_shape=jax.ShapeDtypeStruct((m, d), F32),
        grid_spec=pltpu.PrefetchScalarGridSpec(
            num_scalar_prefetch=1,
            grid=(m // TM, d // TN_OUT),
            in_specs=[
                pl.BlockSpec((TM, kdim), lambda i, n, s: (i, 0)),
                pl.BlockSpec((TM, d), lambda i, n, s: (i, 0)),
                mod_spec, gain_spec,
                pl.BlockSpec((None, kdim, TN_OUT), lambda i, n, s: (s[0], 0, n)),
            ],
            out_specs=pl.BlockSpec((TM, d), lambda i, n, s: (i, 0)),
            scratch_shapes=[pltpu.VMEM((TM, LANES), F32)]),
        compiler_params=_params(("arbitrary", "arbitrary"), V7X_VMEM_LIMIT_BYTES),
        name="out_projection",
    )(sel, a, x, mod, norm_post, w_out)


def _rope_tables(n_ctx_rows, lat_batch, lat_seq):
    t = np.arange(lat_seq)
    row = (t // GRID_W).astype(np.float32)
    col = (t % GRID_W).astype(np.float32)

    def tables(dim):
        nf = dim // 4
        inv = jnp.asarray(ROPE_THETA, F32) ** (-jnp.arange(nf, dtype=F32) / nf)
        ang = jnp.concatenate([row[:, None] * inv, col[:, None] * inv], axis=-1)
        cos, sin = jnp.cos(ang), jnp.sin(ang)
        cos_t = jnp.tile(jnp.concatenate([cos, cos], axis=-1), (1, LANES // dim))
        sin_t = jnp.tile(jnp.concatenate([-sin, sin], axis=-1), (1, LANES // dim))
        return (jnp.concatenate([jnp.ones((n_ctx_rows, LANES), F32)] + [cos_t] * lat_batch, axis=0),
                jnp.concatenate([jnp.zeros((n_ctx_rows, LANES), F32)] + [sin_t] * lat_batch, axis=0))

    ca, sa = tables(HEAD_DIM)
    cb, sb = tables(B_QK_DIM)
    return ca, sa, cb, sb


def kernel(x_prompt, x_sample, cache_a_k, cache_a_v, cache_b_k, cache_b_v, cache_c_k, cache_c_v,
           c, c_ctx, w_mod, b_mod, norm_pre, norm_post, ffn_gate, ffn_up, ffn_down,
           w_in, w_out, a_sink, b_lambda, b_subln, c_qnorm, c_knorm):
    batch, seq, d = x_prompt.shape
    lat_batch, lat_seq, _ = x_sample.shape
    depth = w_mod.shape[0]
    past = cache_a_k.shape[2]
    n_ctx_rows = batch * seq
    assert d == D_MODEL and depth == DEPTH and w_in.shape[-1] == IN_WIDTH
    assert n_ctx_rows % TM == 0 and lat_seq % TM == 0 and 1 + lat_batch <= N_COND
    assert n_ctx_rows % lat_seq == 0 and lat_seq % TQ_LAT == 0 and TM % seq == 0

    x = jnp.concatenate([x_prompt.reshape(n_ctx_rows, d), x_sample.reshape(lat_batch * lat_seq, d)], axis=0)
    cond = jnp.concatenate([c_ctx[None, :], c, jnp.zeros((N_COND - 1 - lat_batch, d), F32)], axis=0)
    mod = _modulation(cond, w_mod, b_mod).reshape(depth, N_COND, N_SUB, 3, d)
    gpre = norm_pre.reshape(depth, N_SUB, 1, d)
    gpost = norm_post.reshape(depth, N_SUB, 1, d)
    tables = _rope_tables(n_ctx_rows, lat_batch, lat_seq)
    w_in = w_in.astype(BF16)
    w_out = w_out.astype(BF16)
    caches = [a.reshape(lat_batch, depth, -1, HEAD_DIM) for a in
              (cache_a_k, cache_a_v, cache_b_k, cache_b_v, cache_c_k, cache_c_v)]
    lam_init = jnp.asarray([0.8 - 0.6 * math.exp(-0.3 * l) for l in range(depth)], F32)
    subln = b_subln.reshape(depth, 1, HEAD_DIM)
    qnorm = c_qnorm.reshape(depth, 1, HEAD_DIM)
    knorm = c_knorm.reshape(depth, 1, HEAD_DIM)
    geom = dict(n_ctx_rows=n_ctx_rows, lat_rows=lat_seq)

    new_caches = None
    for l in range(depth):
        sel = lambda sub, which=0: jnp.asarray([l, sub, which], jnp.int32)
        x = _ffn(sel(0, 0), x, mod, gpre, gpost, ffn_gate, ffn_up, ffn_down, **geom)
        qkv, *new_caches = _in_proj(sel(1), x, mod, gpre, w_in, tables, qnorm, knorm, new_caches,
                                    batch=batch, seq=seq, **geom)
        o = _ctx_attention(sel(1), qkv, a_sink, lam_init, b_lambda, subln, batch=batch, seq=seq)
        o = _lat_attention(sel(1), o, qkv, caches, a_sink, lam_init, b_lambda, subln,
                           n_ctx_rows=n_ctx_rows, lat_batch=lat_batch, lat_seq=lat_seq)
        x = _out_proj(sel(1), o, x, mod, gpost, w_out, **geom)
        x = _ffn(sel(2, 1), x, mod, gpre, gpost, ffn_gate, ffn_up, ffn_down, **geom)

    new_kv = [a.reshape(batch, depth, seq, heads, HEAD_DIM) for a, heads in zip(new_caches, _CACHE_HEADS)]

    y_prompt = x[:n_ctx_rows].reshape(batch, seq, d)
    y_sample = x[n_ctx_rows:].reshape(lat_batch, lat_seq, d)
    return (y_prompt, y_sample) + tuple(new_kv)
```

```python
import functools
import math

import jax
import jax.numpy as jnp
import numpy as np
from jax import lax
from jax.experimental import pallas as pl
from jax.experimental.pallas import tpu as pltpu

D_MODEL = 2048
DEPTH = 4
GRID_W = 64
HEAD_DIM = 128
A_HEADS, A_KV = 6, 2
B_HEADS, B_QK_DIM = 4, 64
C_HEADS, C_KV = 6, 2
D_FF = 5632
WINDOW = 128
ROPE_THETA = 10000.0
EPS = 1e-6
FFN_RESID = 0.5
N_SUB = 3
N_COND = 8
Q_WIDTH = (A_HEADS + B_HEADS + C_HEADS) * HEAD_DIM
KV_HEADS = A_KV + B_HEADS + C_KV
KV_WIDTH = KV_HEADS * HEAD_DIM
IN_WIDTH = Q_WIDTH + 2 * KV_WIDTH
LOG2_E = math.log2(math.e)
_OFF = {}
_o = 0
for _name, _w in (("qa", A_HEADS), ("ka", A_KV), ("va", A_KV), ("qb", B_HEADS), ("kb", B_HEADS),
                  ("vb", B_HEADS), ("qc", C_HEADS), ("kc", C_KV), ("vc", C_KV)):
    _OFF[_name] = _o
    _o += _w * HEAD_DIM

V7X_VMEM_LIMIT_BYTES = 56 * 1024 * 1024
LANES = 128

TM = 1024
TF = 256
TN_IN = 256
TN_OUT = 1024
TN_MOD = 2048
ROW_CHUNK = 32
TQ_LAT = 512

F32 = jnp.float32
BF16 = jnp.bfloat16


def _rms(x):
    return x * lax.rsqrt(jnp.mean(x * x, axis=-1, keepdims=True) + EPS)


def _dot(a, b):
    return jnp.dot(a, b, preferred_element_type=F32)


def _dot_t(a, b):
    return lax.dot_general(a, b, (((1,), (1,)), ((), ())), preferred_element_type=F32)


def _cond_of_tile(i, tm, n_ctx_rows, lat_rows):
    return jnp.maximum((i * tm - n_ctx_rows) // lat_rows + 1, 0)


def _params(semantics, vmem_bytes):
    return pltpu.CompilerParams(dimension_semantics=semantics, vmem_limit_bytes=vmem_bytes)


def _mod_kernel(cond_ref, w_ref, b_ref, o_ref):
    c = cond_ref[...]
    s = (c * jax.nn.sigmoid(c)).astype(BF16)
    o_ref[...] = _dot(s, w_ref[...].astype(BF16)) + b_ref[...]


def _modulation(cond, w_mod, b_mod):
    depth, d, n = w_mod.shape
    return pl.pallas_call(
        _mod_kernel,
        out_shape=jax.ShapeDtypeStruct((depth, N_COND, n), F32),
        grid=(depth, n // TN_MOD),
        in_specs=[
            pl.BlockSpec((N_COND, d), lambda l, j: (0, 0)),
            pl.BlockSpec((None, d, TN_MOD), lambda l, j: (l, 0, j)),
            pl.BlockSpec((None, 1, TN_MOD), lambda l, j: (l, 0, j)),
        ],
        out_specs=pl.BlockSpec((None, N_COND, TN_MOD), lambda l, j: (l, 0, j)),
        compiler_params=_params(("arbitrary", "arbitrary"), V7X_VMEM_LIMIT_BYTES),
        name="adaln_modulation",
    )(cond, w_mod, b_mod.reshape(depth, 1, n))


def _row_chunks(n_rows, body, unroll):
    def step(r, carry):
        body(pl.ds(pl.multiple_of(r * ROW_CHUNK, ROW_CHUNK), ROW_CHUNK))
        return carry

    lax.fori_loop(0, n_rows // ROW_CHUNK, step, 0, unroll=unroll)


def _row_rsqrt_mean_square(src_ref, rs_ref):
    def body(rows):
        x = src_ref[rows, :]
        ms = jnp.mean(x * x, axis=-1, keepdims=True)
        rs_ref[rows, :] = jnp.broadcast_to(lax.rsqrt(ms + EPS), (ROW_CHUNK, LANES))

    _row_chunks(src_ref.shape[0], body, unroll=8)


def _modulated_norm_rows(x_ref, mod_ref, gpre_ref, h_ref, rs_ref):
    _row_rsqrt_mean_square(x_ref, rs_ref)
    shift = mod_ref[0:1, :]
    gain = gpre_ref[...] * (1.0 + mod_ref[1:2, :])

    def body(rows):
        rs = rs_ref[rows, :]
        for t in range(x_ref.shape[1] // LANES):
            cols = slice(t * LANES, (t + 1) * LANES)
            h_ref[rows, cols] = (x_ref[rows, cols] * rs * gain[:, cols] + shift[:, cols]).astype(BF16)

    _row_chunks(x_ref.shape[0], body, unroll=2)


def _gated_residual_rows(x_ref, y_ref, o_ref, mod_ref, gpost_ref, rs_ref, weight):
    _row_rsqrt_mean_square(y_ref, rs_ref)
    gain = (weight * mod_ref[2:3, :]) * gpost_ref[...]

    def body(rows):
        rs = rs_ref[rows, :]
        for t in range(x_ref.shape[1] // LANES):
            cols = slice(t * LANES, (t + 1) * LANES)
            o_ref[rows, cols] = x_ref[rows, cols] + y_ref[rows, cols] * rs * gain[:, cols]

    _row_chunks(x_ref.shape[0], body, unroll=2)


def _sublayer_specs(d, cond, n_grid):
    def mod_map(*args):
        i, s = args[0], args[n_grid]
        return (s[0], cond(i), s[1], 0, 0)

    def gain_map(*args):
        s = args[n_grid]
        return (s[0], s[1], 0, 0)

    return (pl.BlockSpec((None, None, None, 3, d), mod_map),
            pl.BlockSpec((None, None, 1, d), gain_map))


def _ffn_kernel(s_ref, x_ref, mod_ref, gpre_ref, gpost_ref, wg_ref, wu_ref, wd_ref, o_ref, h_ref, rs_ref):
    f = pl.program_id(1)

    @pl.when(f == 0)
    def _():
        _modulated_norm_rows(x_ref, mod_ref, gpre_ref, h_ref, rs_ref)
        o_ref[...] = jnp.zeros_like(o_ref)

    h = h_ref[...]
    g = _dot(h, wg_ref[...].astype(BF16))
    u = _dot(h, wu_ref[...].astype(BF16))
    a = (g * jax.nn.sigmoid(g) * u).astype(BF16)
    wd = wd_ref[...].astype(BF16)
    nchunk = 512
    for n in range(0, D_MODEL, nchunk):
        o_ref[:, n:n + nchunk] += _dot(a, wd[:, n:n + nchunk])

    @pl.when(f == pl.num_programs(1) - 1)
    def _():
        _gated_residual_rows(x_ref, o_ref, o_ref, mod_ref, gpost_ref, rs_ref, FFN_RESID)


def _ffn(sel, x, mod, norm_pre, norm_post, w_gate, w_up, w_down, *, n_ctx_rows, lat_rows):
    m, d = x.shape
    cond = functools.partial(_cond_of_tile, tm=TM, n_ctx_rows=n_ctx_rows, lat_rows=lat_rows)
    mod_spec, gain_spec = _sublayer_specs(d, cond, 2)
    return pl.pallas_call(
        _ffn_kernel,
        out_shape=jax.ShapeDtypeStruct((m, d), F32),
        grid_spec=pltpu.PrefetchScalarGridSpec(
            num_scalar_prefetch=1,
            grid=(m // TM, D_FF // TF),
            in_specs=[
                pl.BlockSpec((TM, d), lambda i, f, s: (i, 0)),
                mod_spec, gain_spec, gain_spec,
                pl.BlockSpec((None, None, d, TF), lambda i, f, s: (s[0], s[2], 0, f)),
                pl.BlockSpec((None, None, d, TF), lambda i, f, s: (s[0], s[2], 0, f)),
                pl.BlockSpec((None, None, TF, d), lambda i, f, s: (s[0], s[2], f, 0)),
            ],
            out_specs=pl.BlockSpec((TM, d), lambda i, f, s: (i, 0)),
            scratch_shapes=[pltpu.VMEM((TM, d), BF16), pltpu.VMEM((TM, LANES), F32)]),
        compiler_params=_params(("arbitrary", "arbitrary"), V7X_VMEM_LIMIT_BYTES),
        name="ffn_sublayer",
    )(sel, x, mod, norm_pre, norm_post, w_gate, w_up, w_down)


_ROPE_NONE, _ROPE_HD, _ROPE_QK = 0, 1, 2
_NORM_NONE, _NORM_Q, _NORM_K = 0, 1, 2
_SCALE_ONE, _SCALE_HD, _SCALE_QK = 0, 1, 2


def _qkv_steps():
    blk = lambda name, j: _OFF[name] // TN_IN + j
    q0, k0, v0 = 0, Q_WIDTH // TN_IN, (Q_WIDTH + KV_WIDTH) // TN_IN
    kv = [(blk("ka", 0), k0 + 0, _ROPE_HD, _NORM_NONE), (blk("va", 0), v0 + 0, _ROPE_NONE, _NORM_NONE),
          (blk("kb", 0), k0 + 1, _ROPE_QK, _NORM_NONE), (blk("kb", 1), k0 + 2, _ROPE_QK, _NORM_NONE),
          (blk("vb", 0), v0 + 1, _ROPE_NONE, _NORM_NONE), (blk("vb", 1), v0 + 2, _ROPE_NONE, _NORM_NONE),
          (blk("kc", 0), k0 + 3, _ROPE_HD, _NORM_K), (blk("vc", 0), v0 + 3, _ROPE_NONE, _NORM_NONE)]
    rows = [(src, dst, rope, norm, _SCALE_ONE) for src, dst, rope, norm in kv]
    rows += ([(blk("qa", j), q0 + j, _ROPE_HD, _NORM_NONE, _SCALE_HD) for j in range(3)]
             + [(blk("qb", j), q0 + 3 + j, _ROPE_QK, _NORM_NONE, _SCALE_QK) for j in range(2)]
             + [(blk("qc", j), q0 + 5 + j, _ROPE_HD, _NORM_Q, _SCALE_HD) for j in range(3)])
    return np.asarray(rows, np.int32)


_QKV_STEPS = _qkv_steps()
_CACHE_HEADS = (A_KV, A_KV, B_HEADS, B_HEADS, C_KV, C_KV)
_KV_STEP_CACHE = ((0, 0), (1, 0), (2, 0), (2, 2), (3, 0), (3, 2), (4, 0), (5, 0))
_ST_SRC, _ST_DST, _ST_ROPE, _ST_NORM, _ST_SCALE = range(5)
N_QKV_STEPS = len(_QKV_STEPS)
QKV_SUB_ROWS = 256


def _in_proj_kernel(s_ref, st_ref, *refs, n_ctx_tiles, has_prev_cache):
    n_caches = len(_CACHE_HEADS)
    if has_prev_cache:
        refs = refs[n_caches:]
    (x_ref, mod_ref, gpre_ref, w_ref, ca_ref, sa_ref, cb_ref, sb_ref, qn_ref, kn_ref, qkv_ref) = refs[:11]
    cache_refs = refs[11:11 + n_caches]
    h_ref, rs_ref, n_ref, pa_ref, pb_ref = refs[11 + n_caches:]
    i, c = pl.program_id(0), pl.program_id(1)

    @pl.when(c == 0)
    def _():
        _modulated_norm_rows(x_ref, mod_ref, gpre_ref, h_ref, rs_ref)

    def project(p_ref):
        p_ref[...] = _dot(h_ref[...], w_ref[...])

    b = jnp.maximum(c - 1, 0)
    rope, norm, scale_kind = st_ref[b, _ST_ROPE], st_ref[b, _ST_NORM], st_ref[b, _ST_SCALE]

    def finish(p_ref, rope_kind, use_norm):
        gain = jnp.where(norm == _NORM_Q, qn_ref[...], kn_ref[...])
        scale = jnp.where(scale_kind == _SCALE_HD, HEAD_DIM ** -0.5 * LOG2_E,
                          jnp.where(scale_kind == _SCALE_QK, B_QK_DIM ** -0.5 * LOG2_E, 1.0))
        lane = lax.broadcasted_iota(jnp.int32, (1, LANES), 1)
        low_half = (lane % B_QK_DIM) < (B_QK_DIM // 2)
        cos_ref, sin_ref = {_ROPE_NONE: (None, None), _ROPE_HD: (ca_ref, sa_ref),
                            _ROPE_QK: (cb_ref, sb_ref)}[rope_kind]
        for r in range(p_ref.shape[0] // QKV_SUB_ROWS):
            rows = slice(r * QKV_SUB_ROWS, (r + 1) * QKV_SUB_ROWS)
            for hh in range(TN_IN // HEAD_DIM):
                cols = slice(hh * HEAD_DIM, (hh + 1) * HEAD_DIM)
                n = p_ref[rows, cols]
                if use_norm:
                    n = _rms(n) * gain
                n_ref[rows, cols] = n
                if rope_kind == _ROPE_HD:
                    partner = pltpu.roll(n, HEAD_DIM // 2, axis=1)
                elif rope_kind == _ROPE_QK:
                    partner = jnp.where(low_half, pltpu.roll(n, LANES - B_QK_DIM // 2, axis=1),
                                        pltpu.roll(n, B_QK_DIM // 2, axis=1))
                if rope_kind != _ROPE_NONE:
                    n = n * cos_ref[rows, :] + partner * sin_ref[rows, :]
                qkv_ref[rows, cols] = (n * scale).astype(BF16)

    recipes = sorted({(int(r[_ST_ROPE]), int(r[_ST_NORM]) != _NORM_NONE) for r in _QKV_STEPS})

    def finish_by_recipe(when, project_ref, finish_ref):
        for rope_kind, use_norm in recipes:
            @pl.when(when & (rope == rope_kind) & ((norm != _NORM_NONE) == use_norm))
            def _():
                if project_ref is not None:
                    project(project_ref)
                finish(finish_ref, rope_kind, use_norm)

    odd = c % 2 == 1
    inner = (c > 0) & (c < N_QKV_STEPS)

    @pl.when(c == 0)
    def _():
        project(pa_ref)

    finish_by_recipe(inner & odd, pb_ref, pa_ref)
    finish_by_recipe(inner & jnp.logical_not(odd), pa_ref, pb_ref)
    finish_by_recipe(c == N_QKV_STEPS, None, pa_ref if N_QKV_STEPS % 2 == 1 else pb_ref)

    for step, (cache, head0) in enumerate(_KV_STEP_CACHE):
        @pl.when((c == step + 1) & (i < n_ctx_tiles))
        def _():
            cache_ref, heads = cache_refs[cache], _CACHE_HEADS[cache]
            seq = cache_ref.shape[1] // heads
            for bb in range(cache_ref.shape[0]):
                for hh in range(TN_IN // HEAD_DIM):
                    cache_ref[bb, pl.ds(head0 + hh, seq, stride=heads), :] = (
                        n_ref[bb * seq:(bb + 1) * seq, hh * HEAD_DIM:(hh + 1) * HEAD_DIM])


def _in_proj(sel, x, mod, norm_pre, w_in, tables, c_qnorm, c_knorm, prev_cache, *,
             n_ctx_rows, lat_rows, batch, seq):
    m, d = x.shape
    depth = w_in.shape[0]
    n_ctx_tiles = n_ctx_rows // TM
    tile_batches = TM // seq
    cond = functools.partial(_cond_of_tile, tm=TM, n_ctx_rows=n_ctx_rows, lat_rows=lat_rows)
    mod_spec, gain_spec = _sublayer_specs(d, cond, 2)
    table = pl.BlockSpec((TM, LANES), lambda i, c, s, st: (i, 0))
    head_gain = pl.BlockSpec((None, 1, HEAD_DIM), lambda i, c, s, st: (s[0], 0, 0))
    projected = lambda c: jnp.minimum(c, N_QKV_STEPS - 1)
    finished = lambda c: jnp.maximum(c - 1, 0)

    def cache_spec(heads):
        return pl.BlockSpec((tile_batches, None, seq * heads, HEAD_DIM),
                            lambda i, c, s, st: (jnp.minimum(i, n_ctx_tiles - 1), s[0], 0, 0))

    has_prev = prev_cache is not None
    n_caches = len(_CACHE_HEADS)
    prev_specs = [pl.BlockSpec(memory_space=pl.ANY)] * n_caches if has_prev else []
    prev_args = list(prev_cache) if has_prev else []
    p_buffer = pltpu.VMEM((TM, TN_IN), F32)
    return pl.pallas_call(
        functools.partial(_in_proj_kernel, n_ctx_tiles=n_ctx_tiles, has_prev_cache=has_prev),
        out_shape=(jax.ShapeDtypeStruct((m, IN_WIDTH), BF16),)
                  + tuple(jax.ShapeDtypeStruct((batch, depth, seq * heads, HEAD_DIM), F32)
                          for heads in _CACHE_HEADS),
        grid_spec=pltpu.PrefetchScalarGridSpec(
            num_scalar_prefetch=2,
            grid=(m // TM, N_QKV_STEPS + 1),
            in_specs=prev_specs + [
                pl.BlockSpec((TM, d), lambda i, c, s, st: (i, 0)),
                mod_spec, gain_spec,
                pl.BlockSpec((None, d, TN_IN), lambda i, c, s, st: (s[0], 0, st[projected(c), _ST_SRC])),
                table, table, table, table, head_gain, head_gain,
            ],
            out_specs=(pl.BlockSpec((TM, TN_IN), lambda i, c, s, st: (i, st[finished(c), _ST_DST])),)
                      + tuple(cache_spec(heads) for heads in _CACHE_HEADS),
            scratch_shapes=[pltpu.VMEM((TM, d), BF16), pltpu.VMEM((TM, LANES), F32),
                            pltpu.VMEM((TM, TN_IN), F32), p_buffer, p_buffer]),
        input_output_aliases={2 + k: 1 + k for k in range(n_caches)} if has_prev else {},
        compiler_params=_params(("arbitrary", "arbitrary"), V7X_VMEM_LIMIT_BYTES),
        name="qkv_projection",
    )(sel, jnp.asarray(_QKV_STEPS), *prev_args, x, mod, norm_pre, w_in, *tables, c_qnorm, c_knorm)


VX = 2 * HEAD_DIM


def _store_values_with_ones(v, vx_ref, n_heads, head0=0):
    ones = jnp.ones((v.shape[0], HEAD_DIM), BF16)
    for h in range(n_heads):
        c0 = (head0 + h) * VX
        vx_ref[:, c0:c0 + HEAD_DIM] = v[:, h * HEAD_DIM:(h + 1) * HEAD_DIM]
        vx_ref[:, c0 + HEAD_DIM:c0 + VX] = ones


def _key_loader(ref, rows, head0):
    return lambda h: ref[rows, (head0 + h) * HEAD_DIM:(head0 + h + 1) * HEAD_DIM]


def _value_loader(ref, rows, head0):
    return lambda h: ref[rows, (head0 + h) * VX:(head0 + h + 1) * VX]


_KV_HEAD0 = {"a": 0, "b": A_KV, "c": A_KV + B_HEADS}


def _exp_weighted(scores, values, floor=None):
    m = functools.reduce(jnp.maximum, [jnp.max(s, axis=-1, keepdims=True) for s in scores])
    if floor is not None:
        m = jnp.maximum(m, floor)
    acc = functools.reduce(lambda a, b: a + b,
                           [_dot(jnp.exp2(s - m).astype(BF16), v) for s, v in zip(scores, values)])
    return acc, m


def _attend(q_ref, segs, o_ref, sink_ref, lam_init_ref, bl_ref, subln_ref, layer, local_mask=None):
    lam_init = lam_init_ref[layer]
    bl = bl_ref[...]
    s1 = jnp.sum(bl[0:1, :] * bl[1:2, :], axis=-1, keepdims=True)
    s2 = jnp.sum(bl[2:3, :] * bl[3:4, :], axis=-1, keepdims=True)
    lam = jnp.exp(s1) - jnp.exp(s2) + lam_init
    g_sub = subln_ref[...] * (1.0 - lam_init)
    lane = lax.broadcasted_iota(jnp.int32, (1, HEAD_DIM), 1)
    first_map = lane < B_QK_DIM

    def split(acc):
        return acc[:, :HEAD_DIM], acc[:, HEAD_DIM:]

    for i in range(A_HEADS):
        kv = i // (A_HEADS // A_KV)
        q = q_ref[:, i * HEAD_DIM:(i + 1) * HEAD_DIM]
        scores = [_dot_t(q, k(kv)) for k, _ in segs["a"]]
        if local_mask is not None:
            scores[-1] = jnp.where(local_mask, scores[-1], -jnp.inf)
        sink = sink_ref[layer, i] * LOG2_E
        acc, m = _exp_weighted(scores, [v(kv) for _, v in segs["a"]], floor=sink)
        num, den = split(acc)
        o = num / (den + jnp.exp2(sink - m))
        o_ref[:, i * HEAD_DIM:(i + 1) * HEAD_DIM] = o.astype(o_ref.dtype)

    for i in range(B_HEADS):
        c0 = (A_HEADS + i) * HEAD_DIM
        q = q_ref[:, c0:c0 + HEAD_DIM]
        q1 = jnp.where(first_map, q, jnp.zeros_like(q))
        q2 = jnp.where(first_map, jnp.zeros_like(q), q)
        values = [v(i) for _, v in segs["b"]]
        n1, d1 = split(_exp_weighted([_dot_t(q1, k(i)) for k, _ in segs["b"]], values)[0])
        n2, d2 = split(_exp_weighted([_dot_t(q2, k(i)) for k, _ in segs["b"]], values)[0])
        o = n1 / d1 - lam * (n2 / d2)
        o = _rms(o) * g_sub
        o_ref[:, c0:c0 + HEAD_DIM] = o.astype(o_ref.dtype)

    for i in range(C_HEADS):
        kv = i // (C_HEADS // C_KV)
        c0 = (A_HEADS + B_HEADS + i) * HEAD_DIM
        q = q_ref[:, c0:c0 + HEAD_DIM]
        acc, _ = _exp_weighted([_dot_t(q, k(kv)) for k, _ in segs["c"]], [v(kv) for _, v in segs["c"]])
        num, den = split(acc)
        o_ref[:, c0:c0 + HEAD_DIM] = (num / den).astype(o_ref.dtype)


def _ctx_attn_kernel(s_ref, q_ref, k_ref, v_ref, sink_ref, lam_init_ref, bl_ref, subln_ref, o_ref, vx_ref):
    _store_values_with_ones(v_ref[...], vx_ref, KV_HEADS)
    rows = slice(None)
    segs = {g: [(_key_loader(k_ref, rows, h0), _value_loader(vx_ref, rows, h0))]
            for g, h0 in _KV_HEAD0.items()}
    _attend(q_ref, segs, o_ref, sink_ref, lam_init_ref, bl_ref, subln_ref, s_ref[0])


def _attn_param_specs(n_grid):
    def layer_map(*args):
        return (args[n_grid][0], 0, 0)

    smem = pl.BlockSpec(memory_space=pltpu.SMEM)
    return [smem, smem,
            pl.BlockSpec((None, 4, B_QK_DIM), layer_map),
            pl.BlockSpec((None, 1, HEAD_DIM), layer_map)]


_K_COL_BLOCK = Q_WIDTH // KV_WIDTH
_V_COL_BLOCK = _K_COL_BLOCK + 1


def _ctx_attention(sel, qkv, a_sink, lam_init, b_lambda, b_subln, *, batch, seq):
    return pl.pallas_call(
        _ctx_attn_kernel,
        out_shape=jax.ShapeDtypeStruct((qkv.shape[0], Q_WIDTH), BF16),
        grid_spec=pltpu.PrefetchScalarGridSpec(
            num_scalar_prefetch=1,
            grid=(batch,),
            in_specs=[pl.BlockSpec((seq, Q_WIDTH), lambda b, s: (b, 0)),
                      pl.BlockSpec((seq, KV_WIDTH), lambda b, s: (b, _K_COL_BLOCK)),
                      pl.BlockSpec((seq, KV_WIDTH), lambda b, s: (b, _V_COL_BLOCK))] + _attn_param_specs(1),
            out_specs=pl.BlockSpec((seq, Q_WIDTH), lambda b, s: (b, 0)),
            scratch_shapes=[pltpu.VMEM((seq, KV_HEADS * VX), BF16)]),
        compiler_params=_params(("arbitrary",), 40 * 1024 * 1024),
        name="context_attention",
    )(sel, qkv, qkv, qkv, a_sink, lam_init, b_lambda, b_subln)


def _lat_attn_kernel(s_ref, o_in_ref, q_ref, k_ref, v_ref, cak_ref, cav_ref, cbk_ref, cbv_ref, cck_ref, ccv_ref,
                     sink_ref, lam_init_ref, bl_ref, subln_ref, o_ref, ck_ref, cvx_ref, vx_ref, *, lat_seq):
    del o_in_ref
    @pl.when(pl.program_id(1) == 0)
    def _():
        past = ck_ref.shape[0]
        h0 = 0
        for kref, vref in ((cak_ref, cav_ref), (cbk_ref, cbv_ref), (cck_ref, ccv_ref)):
            n_heads = kref.shape[0] // past
            for h in range(n_heads):
                head_rows = pl.ds(h, past, stride=n_heads)
                ck_ref[:, (h0 + h) * HEAD_DIM:(h0 + h + 1) * HEAD_DIM] = kref[head_rows, :].astype(BF16)
                _store_values_with_ones(vref[head_rows, :].astype(BF16), cvx_ref, 1, h0 + h)
            h0 += n_heads
        _store_values_with_ones(v_ref[...], vx_ref, KV_HEADS)

    tq = q_ref.shape[0]
    band = tq + 2 * WINDOW
    q0 = pl.program_id(1) * tq
    k0 = pl.multiple_of(jnp.clip(q0 - WINDOW, 0, lat_seq - band), WINDOW)
    qpos = q0 + lax.broadcasted_iota(jnp.int32, (tq, 1), 0)
    kpos = k0 + lax.broadcasted_iota(jnp.int32, (1, band), 1)
    local_mask = jnp.abs(qpos - kpos) <= WINDOW

    every = slice(None)
    lat_rows = {"a": pl.ds(k0, band), "b": every, "c": every}
    segs = {g: [(_key_loader(ck_ref, every, h0), _value_loader(cvx_ref, every, h0)),
                (_key_loader(k_ref, lat_rows[g], h0), _value_loader(vx_ref, lat_rows[g], h0))]
            for g, h0 in _KV_HEAD0.items()}
    _attend(q_ref, segs, o_ref, sink_ref, lam_init_ref, bl_ref, subln_ref, s_ref[0], local_mask=local_mask)


def _lat_attention(sel, o, qkv, caches, a_sink, lam_init, b_lambda, b_subln, *, n_ctx_rows, lat_batch, lat_seq):
    q_blk0 = n_ctx_rows // TQ_LAT
    kv_blk0 = n_ctx_rows // lat_seq
    n_q = lat_seq // TQ_LAT
    past = caches[0].shape[2] // _CACHE_HEADS[0]

    def cache_spec(a):
        return pl.BlockSpec((None, None) + a.shape[2:], lambda b, i, s: (b, s[0], 0, 0))

    return pl.pallas_call(
        functools.partial(_lat_attn_kernel, lat_seq=lat_seq),
        out_shape=jax.ShapeDtypeStruct(o.shape, o.dtype),
        grid_spec=pltpu.PrefetchScalarGridSpec(
            num_scalar_prefetch=1,
            grid=(lat_batch, n_q),
            in_specs=[pl.BlockSpec(memory_space=pl.ANY),
                      pl.BlockSpec((TQ_LAT, Q_WIDTH), lambda b, i, s: (q_blk0 + b * n_q + i, 0)),
                      pl.BlockSpec((lat_seq, KV_WIDTH), lambda b, i, s: (kv_blk0 + b, _K_COL_BLOCK)),
                      pl.BlockSpec((lat_seq, KV_WIDTH), lambda b, i, s: (kv_blk0 + b, _V_COL_BLOCK))]
                     + [cache_spec(a) for a in caches] + _attn_param_specs(2),
            out_specs=pl.BlockSpec((TQ_LAT, Q_WIDTH), lambda b, i, s: (q_blk0 + b * n_q + i, 0)),
            scratch_shapes=[pltpu.VMEM((past, KV_WIDTH), BF16),
                            pltpu.VMEM((past, KV_HEADS * VX), BF16),
                            pltpu.VMEM((lat_seq, KV_HEADS * VX), BF16)]),
        input_output_aliases={1: 0},
        compiler_params=_params(("arbitrary", "arbitrary"), V7X_VMEM_LIMIT_BYTES),
        name="latent_attention",
    )(sel, o, qkv, qkv, qkv, *caches, a_sink, lam_init, b_lambda, b_subln)


def _out_proj_kernel(s_ref, a_ref, x_ref, mod_ref, gpost_ref, w_ref, o_ref, rs_ref):
    n = pl.program_id(1)
    y = _dot(a_ref[...], w_ref[...])
    for slab in range(o_ref.shape[1] // TN_OUT):
        @pl.when(n == slab)
        def _():
            o_ref[:, slab * TN_OUT:(slab + 1) * TN_OUT] = y

    @pl.when(n == pl.num_programs(1) - 1)
    def _():
        _gated_residual_rows(x_ref, o_ref, o_ref, mod_ref, gpost_ref, rs_ref, 1.0)


def _out_proj(sel, a, x, mod, norm_post, w_out, *, n_ctx_rows, lat_rows):
    m, d = x.shape
    kdim = a.shape[1]
    cond = functools.partial(_cond_of_tile, tm=TM, n_ctx_rows=n_ctx_rows, lat_rows=lat_rows)
    mod_spec, gain_spec = _sublayer_specs(d, cond, 2)
    return pl.pallas_call(
        _out_proj_kernel,
        out_shape=jax.ShapeDtypeStruct((m, d), F32),
        grid_spec=pltpu.PrefetchScalarGridSpec(
            num_scalar_prefetch=1,
            grid=(m // TM, d // TN_OUT),
            in_specs=[
                pl.BlockSpec((TM, kdim), lambda i, n, s: (i, 0)),
                pl.BlockSpec((TM, d), lambda i, n, s: (i, 0)),
                mod_spec, gain_spec,
                pl.BlockSpec((None, kdim, TN_OUT), lambda i, n, s: (s[0], 0, n)),
            ],
            out_specs=pl.BlockSpec((TM, d), lambda i, n, s: (i, 0)),
            scratch_shapes=[pltpu.VMEM((TM, LANES), F32)]),
        compiler_params=_params(("arbitrary", "arbitrary"), V7X_VMEM_LIMIT_BYTES),
        name="out_projection",
    )(sel, a, x, mod, norm_post, w_out)


def _rope_tables(n_ctx_rows, lat_batch, lat_seq):
    t = np.arange(lat_seq)
    row = (t // GRID_W).astype(np.float32)
    col = (t % GRID_W).astype(np.float32)

    def tables(dim):
        nf = dim // 4
        inv = jnp.asarray(ROPE_THETA, F32) ** (-jnp.arange(nf, dtype=F32) / nf)
        ang = jnp.concatenate([row[:, None] * inv, col[:, None] * inv], axis=-1)
        cos, sin = jnp.cos(ang), jnp.sin(ang)
        cos_t = jnp.tile(jnp.concatenate([cos, cos], axis=-1), (1, LANES // dim))
        sin_t = jnp.tile(jnp.concatenate([-sin, sin], axis=-1), (1, LANES // dim))
        return (jnp.concatenate([jnp.ones((n_ctx_rows, LANES), F32)] + [cos_t] * lat_batch, axis=0),
                jnp.concatenate([jnp.zeros((n_ctx_rows, LANES), F32)] + [sin_t] * lat_batch, axis=0))

    ca, sa = tables(HEAD_DIM)
    cb, sb = tables(B_QK_DIM)
    return ca, sa, cb, sb


def kernel(x_prompt, x_sample, cache_a_k, cache_a_v, cache_b_k, cache_b_v, cache_c_k, cache_c_v,
           c, c_ctx, w_mod, b_mod, norm_pre, norm_post, ffn_gate, ffn_up, ffn_down,
           w_in, w_out, a_sink, b_lambda, b_subln, c_qnorm, c_knorm):
    batch, seq, d = x_prompt.shape
    lat_batch, lat_seq, _ = x_sample.shape
    depth = w_mod.shape[0]
    past = cache_a_k.shape[2]
    n_ctx_rows = batch * seq
    assert d == D_MODEL and depth == DEPTH and w_in.shape[-1] == IN_WIDTH
    assert n_ctx_rows % TM == 0 and lat_seq % TM == 0 and 1 + lat_batch <= N_COND
    assert n_ctx_rows % lat_seq == 0 and lat_seq % TQ_LAT == 0 and TM % seq == 0

    x = jnp.concatenate([x_prompt.reshape(n_ctx_rows, d), x_sample.reshape(lat_batch * lat_seq, d)], axis=0)
    cond = jnp.concatenate([c_ctx[None, :], c, jnp.zeros((N_COND - 1 - lat_batch, d), F32)], axis=0)
    mod = _modulation(cond, w_mod, b_mod).reshape(depth, N_COND, N_SUB, 3, d)
    gpre = norm_pre.reshape(depth, N_SUB, 1, d)
    gpost = norm_post.reshape(depth, N_SUB, 1, d)
    tables = _rope_tables(n_ctx_rows, lat_batch, lat_seq)
    w_in = w_in.astype(BF16)
    w_out = w_out.astype(BF16)
    caches = [a.reshape(lat_batch, depth, -1, HEAD_DIM) for a in
              (cache_a_k, cache_a_v, cache_b_k, cache_b_v, cache_c_k, cache_c_v)]
    lam_init = jnp.asarray([0.8 - 0.6 * math.exp(-0.3 * l) for l in range(depth)], F32)
    subln = b_subln.reshape(depth, 1, HEAD_DIM)
    qnorm = c_qnorm.reshape(depth, 1, HEAD_DIM)
    knorm = c_knorm.reshape(depth, 1, HEAD_DIM)
    geom = dict(n_ctx_rows=n_ctx_rows, lat_rows=lat_seq)

    new_caches = None
    for l in range(depth):
        sel = lambda sub, which=0: jnp.asarray([l, sub, which], jnp.int32)
        x = _ffn(sel(0, 0), x, mod, gpre, gpost, ffn_gate, ffn_up, ffn_down, **geom)
        qkv, *new_caches = _in_proj(sel(1), x, mod, gpre, w_in, tables, qnorm, knorm, new_caches,
                                    batch=batch, seq=seq, **geom)
        o = _ctx_attention(sel(1), qkv, a_sink, lam_init, b_lambda, subln, batch=batch, seq=seq)
        o = _lat_attention(sel(1), o, qkv, caches, a_sink, lam_init, b_lambda, subln,
                           n_ctx_rows=n_ctx_rows, lat_batch=lat_batch, lat_seq=lat_seq)
        x = _out_proj(sel(1), o, x, mod, gpost, w_out, **geom)
        x = _ffn(sel(2, 1), x, mod, gpre, gpost, ffn_gate, ffn_up, ffn_down, **geom)

    new_kv = [a.reshape(batch, depth, seq, heads, HEAD_DIM) for a, heads in zip(new_caches, _CACHE_HEADS)]

    y_prompt = x[:n_ctx_rows].reshape(batch, seq, d)
    y_sample = x[n_ctx_rows:].reshape(lat_batch, lat_seq, d)
    return (y_prompt, y_sample) + tuple(new_kv)
```

```python
import functools
import math

import jax
import jax.numpy as jnp
import numpy as np
from jax import lax
from jax.experimental import pallas as pl
from jax.experimental.pallas import tpu as pltpu

D_MODEL = 2048
DEPTH = 4
GRID_W = 64
HEAD_DIM = 128
A_HEADS, A_KV = 6, 2
B_HEADS, B_QK_DIM = 4, 64
C_HEADS, C_KV = 6, 2
D_FF = 5632
WINDOW = 128
ROPE_THETA = 10000.0
EPS = 1e-6
FFN_RESID = 0.5
N_SUB = 3
N_COND = 8
Q_WIDTH = (A_HEADS + B_HEADS + C_HEADS) * HEAD_DIM
KV_HEADS = A_KV + B_HEADS + C_KV
KV_WIDTH = KV_HEADS * HEAD_DIM
IN_WIDTH = Q_WIDTH + 2 * KV_WIDTH
LOG2_E = math.log2(math.e)
_OFF = {}
_o = 0
for _name, _w in (("qa", A_HEADS), ("ka", A_KV), ("va", A_KV), ("qb", B_HEADS), ("kb", B_HEADS),
                  ("vb", B_HEADS), ("qc", C_HEADS), ("kc", C_KV), ("vc", C_KV)):
    _OFF[_name] = _o
    _o += _w * HEAD_DIM

V7X_VMEM_LIMIT_BYTES = 56 * 1024 * 1024
LANES = 128

TM = 1024
TF = 256
TN_IN = 256
TN_OUT = 1024
TN_MOD = 1024
ROW_CHUNK = 32
TQ_LAT = 256

F32 = jnp.float32
BF16 = jnp.bfloat16


def _rms(x):
    return x * lax.rsqrt(jnp.mean(x * x, axis=-1, keepdims=True) + EPS)


def _dot(a, b):
    return jnp.dot(a, b, preferred_element_type=F32)


def _dot_t(a, b):
    return lax.dot_general(a, b, (((1,), (1,)), ((), ())), preferred_element_type=F32)


def _cond_of_tile(i, tm, n_ctx_rows, lat_rows):
    return jnp.maximum((i * tm - n_ctx_rows) // lat_rows + 1, 0)


def _params(semantics, vmem_bytes):
    return pltpu.CompilerParams(dimension_semantics=semantics, vmem_limit_bytes=vmem_bytes)


def _mod_kernel(cond_ref, w_ref, b_ref, o_ref):
    c = cond_ref[...]
    s = (c * jax.nn.sigmoid(c)).astype(BF16)
    o_ref[...] = _dot(s, w_ref[...].astype(BF16)) + b_ref[...]


def _modulation(cond, w_mod, b_mod):
    depth, d, n = w_mod.shape
    return pl.pallas_call(
        _mod_kernel,
        out_shape=jax.ShapeDtypeStruct((depth, N_COND, n), F32),
        grid=(depth, n // TN_MOD),
        in_specs=[
            pl.BlockSpec((N_COND, d), lambda l, j: (0, 0)),
            pl.BlockSpec((None, d, TN_MOD), lambda l, j: (l, 0, j)),
            pl.BlockSpec((None, 1, TN_MOD), lambda l, j: (l, 0, j)),
        ],
        out_specs=pl.BlockSpec((None, N_COND, TN_MOD), lambda l, j: (l, 0, j)),
        compiler_params=_params(("arbitrary", "arbitrary"), 40 * 1024 * 1024),
        name="adaln_modulation",
    )(cond, w_mod, b_mod.reshape(depth, 1, n))


def _row_chunks(n_rows, body, unroll):
    def step(r, carry):
        body(pl.ds(pl.multiple_of(r * ROW_CHUNK, ROW_CHUNK), ROW_CHUNK))
        return carry

    lax.fori_loop(0, n_rows // ROW_CHUNK, step, 0, unroll=unroll)


def _row_rsqrt_mean_square(src_ref, rs_ref):
    def body(rows):
        x = src_ref[rows, :]
        ms = jnp.mean(x * x, axis=-1, keepdims=True)
        rs_ref[rows, :] = jnp.broadcast_to(lax.rsqrt(ms + EPS), (ROW_CHUNK, LANES))

    _row_chunks(src_ref.shape[0], body, unroll=8)


_P_SHIFT, _P_SCALE, _P_GATE, _P_GPRE, _P_GPOST = range(5)
SUB_ROWS = 8


def _sublayer_params(mod, norm_pre, norm_post):
    depth, n_cond, n_sub, _, d = mod.shape
    gains = jnp.stack([norm_pre, norm_post], axis=2)
    gains = jnp.broadcast_to(gains[:, None], (depth, n_cond, n_sub, 2, d))
    pad = jnp.zeros((depth, n_cond, n_sub, SUB_ROWS - 5, d), F32)
    return jnp.concatenate([mod, gains, pad], axis=3)


def _modulated_norm_rows(x_ref, par_ref, h_ref, rs_ref):
    _row_rsqrt_mean_square(x_ref, rs_ref)
    shift = par_ref[_P_SHIFT:_P_SHIFT + 1, :]
    gain = par_ref[_P_GPRE:_P_GPRE + 1, :] * (1.0 + par_ref[_P_SCALE:_P_SCALE + 1, :])

    def body(rows):
        rs = rs_ref[rows, :]
        for t in range(x_ref.shape[1] // LANES):
            cols = slice(t * LANES, (t + 1) * LANES)
            h_ref[rows, cols] = (x_ref[rows, cols] * rs * gain[:, cols] + shift[:, cols]).astype(BF16)

    _row_chunks(x_ref.shape[0], body, unroll=2)


def _gated_residual_rows(x_ref, y_ref, o_ref, par_ref, rs_ref, weight):
    _row_rsqrt_mean_square(y_ref, rs_ref)
    gain = (weight * par_ref[_P_GATE:_P_GATE + 1, :]) * par_ref[_P_GPOST:_P_GPOST + 1, :]

    def body(rows):
        rs = rs_ref[rows, :]
        for t in range(x_ref.shape[1] // LANES):
            cols = slice(t * LANES, (t + 1) * LANES)
            o_ref[rows, cols] = x_ref[rows, cols] + y_ref[rows, cols] * rs * gain[:, cols]

    _row_chunks(x_ref.shape[0], body, unroll=2)


def _sublayer_spec(d, cond, n_grid):
    def par_map(*args):
        i, s = args[0], args[n_grid]
        return (s[0], cond(i), s[1], 0, 0)

    return pl.BlockSpec((None, None, None, SUB_ROWS, d), par_map)


def _ffn_kernel(s_ref, x_ref, par_ref, wg_ref, wu_ref, wd_ref, o_ref, h_ref, rs_ref):
    f = pl.program_id(1)

    @pl.when(f == 0)
    def _():
        _modulated_norm_rows(x_ref, par_ref, h_ref, rs_ref)
        o_ref[...] = jnp.zeros_like(o_ref)

    h = h_ref[...]
    g = _dot(h, wg_ref[...].astype(BF16))
    u = _dot(h, wu_ref[...].astype(BF16))
    a = (g * jax.nn.sigmoid(g) * u).astype(BF16)
    wd = wd_ref[...].astype(BF16)
    nchunk = 512
    for n in range(0, D_MODEL, nchunk):
        o_ref[:, n:n + nchunk] += _dot(a, wd[:, n:n + nchunk])

    @pl.when(f == pl.num_programs(1) - 1)
    def _():
        _gated_residual_rows(x_ref, o_ref, o_ref, par_ref, rs_ref, FFN_RESID)


def _ffn(sel, x, params, w_gate, w_up, w_down, *, n_ctx_rows, lat_rows):
    m, d = x.shape
    cond = functools.partial(_cond_of_tile, tm=TM, n_ctx_rows=n_ctx_rows, lat_rows=lat_rows)
    return pl.pallas_call(
        _ffn_kernel,
        out_shape=jax.ShapeDtypeStruct((m, d), F32),
        grid_spec=pltpu.PrefetchScalarGridSpec(
            num_scalar_prefetch=1,
            grid=(m // TM, D_FF // TF),
            in_specs=[
                pl.BlockSpec((TM, d), lambda i, f, s: (i, 0)),
                _sublayer_spec(d, cond, 2),
                pl.BlockSpec((None, None, d, TF), lambda i, f, s: (s[0], s[2], 0, f)),
                pl.BlockSpec((None, None, d, TF), lambda i, f, s: (s[0], s[2], 0, f)),
                pl.BlockSpec((None, None, TF, d), lambda i, f, s: (s[0], s[2], f, 0)),
            ],
            out_specs=pl.BlockSpec((TM, d), lambda i, f, s: (i, 0)),
            scratch_shapes=[pltpu.VMEM((TM, d), BF16), pltpu.VMEM((TM, LANES), F32)]),
        compiler_params=_params(("arbitrary", "arbitrary"), V7X_VMEM_LIMIT_BYTES),
        name="ffn_sublayer",
    )(sel, x, params, w_gate, w_up, w_down)


_ROPE_NONE, _ROPE_HD, _ROPE_QK = 0, 1, 2
_NORM_NONE, _NORM_Q, _NORM_K = 0, 1, 2
_SCALE_ONE, _SCALE_HD, _SCALE_QK = 0, 1, 2


def _qkv_steps():
    blk = lambda name, j: _OFF[name] // TN_IN + j
    q0, k0, v0 = 0, Q_WIDTH // TN_IN, (Q_WIDTH + KV_WIDTH) // TN_IN
    kv = [(blk("ka", 0), k0 + 0, _ROPE_HD, _NORM_NONE), (blk("va", 0), v0 + 0, _ROPE_NONE, _NORM_NONE),
          (blk("kb", 0), k0 + 1, _ROPE_QK, _NORM_NONE), (blk("kb", 1), k0 + 2, _ROPE_QK, _NORM_NONE),
          (blk("vb", 0), v0 + 1, _ROPE_NONE, _NORM_NONE), (blk("vb", 1), v0 + 2, _ROPE_NONE, _NORM_NONE),
          (blk("kc", 0), k0 + 3, _ROPE_HD, _NORM_K), (blk("vc", 0), v0 + 3, _ROPE_NONE, _NORM_NONE)]
    rows = [(src, dst, rope, norm, _SCALE_ONE) for src, dst, rope, norm in kv]
    rows += ([(blk("qa", j), q0 + j, _ROPE_HD, _NORM_NONE, _SCALE_HD) for j in range(3)]
             + [(blk("qb", j), q0 + 3 + j, _ROPE_QK, _NORM_NONE, _SCALE_QK) for j in range(2)]
             + [(blk("qc", j), q0 + 5 + j, _ROPE_HD, _NORM_Q, _SCALE_HD) for j in range(3)])
    return np.asarray(rows, np.int32)


_QKV_STEPS = _qkv_steps()
_CACHE_HEADS = (A_KV, A_KV, B_HEADS, B_HEADS, C_KV, C_KV)
_KV_STEP_CACHE = ((0, 0), (1, 0), (2, 0), (2, 2), (3, 0), (3, 2), (4, 0), (5, 0))
_ST_SRC, _ST_DST, _ST_ROPE, _ST_NORM, _ST_SCALE = range(5)
N_QKV_STEPS = len(_QKV_STEPS)
QKV_SUB_ROWS = 256


def _in_proj_kernel(s_ref, st_ref, *refs, n_ctx_tiles, has_prev_cache):
    n_caches = len(_CACHE_HEADS)
    if has_prev_cache:
        refs = refs[n_caches:]
    x_ref, par_ref, w_ref, rope_ref, qkn_ref, qkv_ref = refs[:6]
    cache_refs = refs[6:6 + n_caches]
    h_ref, rs_ref, n_ref, pa_ref, pb_ref = refs[6 + n_caches:]
    rope_cols = {_ROPE_HD: (0, LANES), _ROPE_QK: (2 * LANES, 3 * LANES)}
    i, c = pl.program_id(0), pl.program_id(1)

    @pl.when(c == 0)
    def _():
        _modulated_norm_rows(x_ref, par_ref, h_ref, rs_ref)

    def project(p_ref):
        p_ref[...] = _dot(h_ref[...], w_ref[...])

    b = jnp.maximum(c - 1, 0)
    rope, norm, scale_kind = st_ref[b, _ST_ROPE], st_ref[b, _ST_NORM], st_ref[b, _ST_SCALE]

    def finish(p_ref, rope_kind, use_norm):
        gain = jnp.where(norm == _NORM_Q, qkn_ref[0:1, :], qkn_ref[1:2, :])
        scale = jnp.where(scale_kind == _SCALE_HD, HEAD_DIM ** -0.5 * LOG2_E,
                          jnp.where(scale_kind == _SCALE_QK, B_QK_DIM ** -0.5 * LOG2_E, 1.0))
        lane = lax.broadcasted_iota(jnp.int32, (1, LANES), 1)
        low_half = (lane % B_QK_DIM) < (B_QK_DIM // 2)
        cos0, sin0 = rope_cols.get(rope_kind, (0, 0))
        for r in range(p_ref.shape[0] // QKV_SUB_ROWS):
            rows = slice(r * QKV_SUB_ROWS, (r + 1) * QKV_SUB_ROWS)
            for hh in range(TN_IN // HEAD_DIM):
                cols = slice(hh * HEAD_DIM, (hh + 1) * HEAD_DIM)
                n = p_ref[rows, cols]
                if use_norm:
                    n = _rms(n) * gain
                n_ref[rows, cols] = n
                if rope_kind == _ROPE_HD:
                    partner = pltpu.roll(n, HEAD_DIM // 2, axis=1)
                elif rope_kind == _ROPE_QK:
                    partner = jnp.where(low_half, pltpu.roll(n, LANES - B_QK_DIM // 2, axis=1),
                                        pltpu.roll(n, B_QK_DIM // 2, axis=1))
                if rope_kind != _ROPE_NONE:
                    n = (n * rope_ref[rows, cos0:cos0 + LANES]
                         + partner * rope_ref[rows, sin0:sin0 + LANES])
                qkv_ref[rows, cols] = (n * scale).astype(BF16)

    recipes = sorted({(int(r[_ST_ROPE]), int(r[_ST_NORM]) != _NORM_NONE) for r in _QKV_STEPS})

    def finish_by_recipe(when, project_ref, finish_ref):
        for rope_kind, use_norm in recipes:
            @pl.when(when & (rope == rope_kind) & ((norm != _NORM_NONE) == use_norm))
            def _():
                if project_ref is not None:
                    project(project_ref)
                finish(finish_ref, rope_kind, use_norm)

    odd = c % 2 == 1
    inner = (c > 0) & (c < N_QKV_STEPS)

    @pl.when(c == 0)
    def _():
        project(pa_ref)

    finish_by_recipe(inner & odd, pb_ref, pa_ref)
    finish_by_recipe(inner & jnp.logical_not(odd), pa_ref, pb_ref)
    finish_by_recipe(c == N_QKV_STEPS, None, pa_ref if N_QKV_STEPS % 2 == 1 else pb_ref)

    for step, (cache, head0) in enumerate(_KV_STEP_CACHE):
        @pl.when((c == step + 1) & (i < n_ctx_tiles))
        def _():
            cache_ref, heads = cache_refs[cache], _CACHE_HEADS[cache]
            seq = cache_ref.shape[1] // heads
            for bb in range(cache_ref.shape[0]):
                for hh in range(TN_IN // HEAD_DIM):
                    cache_ref[bb, pl.ds(head0 + hh, seq, stride=heads), :] = (
                        n_ref[bb * seq:(bb + 1) * seq, hh * HEAD_DIM:(hh + 1) * HEAD_DIM])


def _in_proj(sel, x, params, w_in, rope, qk_gains, prev_cache, *, n_ctx_rows, lat_rows, batch, seq):
    m, d = x.shape
    depth = w_in.shape[0]
    n_ctx_tiles = n_ctx_rows // TM
    tile_batches = TM // seq
    cond = functools.partial(_cond_of_tile, tm=TM, n_ctx_rows=n_ctx_rows, lat_rows=lat_rows)
    projected = lambda c: jnp.minimum(c, N_QKV_STEPS - 1)
    finished = lambda c: jnp.maximum(c - 1, 0)

    def cache_spec(heads):
        return pl.BlockSpec((tile_batches, None, seq * heads, HEAD_DIM),
                            lambda i, c, s, st: (jnp.minimum(i, n_ctx_tiles - 1), s[0], 0, 0))

    has_prev = prev_cache is not None
    n_caches = len(_CACHE_HEADS)
    prev_specs = [pl.BlockSpec(memory_space=pl.ANY)] * n_caches if has_prev else []
    prev_args = list(prev_cache) if has_prev else []
    p_buffer = pltpu.VMEM((TM, TN_IN), F32)
    return pl.pallas_call(
        functools.partial(_in_proj_kernel, n_ctx_tiles=n_ctx_tiles, has_prev_cache=has_prev),
        out_shape=(jax.ShapeDtypeStruct((m, IN_WIDTH), BF16),)
                  + tuple(jax.ShapeDtypeStruct((batch, depth, seq * heads, HEAD_DIM), F32)
                          for heads in _CACHE_HEADS),
        grid_spec=pltpu.PrefetchScalarGridSpec(
            num_scalar_prefetch=2,
            grid=(m // TM, N_QKV_STEPS + 1),
            in_specs=prev_specs + [
                pl.BlockSpec((TM, d), lambda i, c, s, st: (i, 0)),
                _sublayer_spec(d, cond, 2),
                pl.BlockSpec((None, d, TN_IN), lambda i, c, s, st: (s[0], 0, st[projected(c), _ST_SRC])),
                pl.BlockSpec((TM, rope.shape[1]), lambda i, c, s, st: (i, 0)),
                pl.BlockSpec((None, 2, HEAD_DIM), lambda i, c, s, st: (s[0], 0, 0)),
            ],
            out_specs=(pl.BlockSpec((TM, TN_IN), lambda i, c, s, st: (i, st[finished(c), _ST_DST])),)
                      + tuple(cache_spec(heads) for heads in _CACHE_HEADS),
            scratch_shapes=[pltpu.VMEM((TM, d), BF16), pltpu.VMEM((TM, LANES), F32),
                            pltpu.VMEM((TM, TN_IN), F32), p_buffer, p_buffer]),
        input_output_aliases={2 + k: 1 + k for k in range(n_caches)} if has_prev else {},
        compiler_params=_params(("arbitrary", "arbitrary"), V7X_VMEM_LIMIT_BYTES),
        name="qkv_projection",
    )(sel, jnp.asarray(_QKV_STEPS), *prev_args, x, params, w_in, rope, qk_gains)


VX = 2 * HEAD_DIM


def _store_values_with_ones(v, vx_ref, n_heads, head0=0):
    ones = jnp.ones((v.shape[0], HEAD_DIM), BF16)
    for h in range(n_heads):
        c0 = (head0 + h) * VX
        vx_ref[:, c0:c0 + HEAD_DIM] = v[:, h * HEAD_DIM:(h + 1) * HEAD_DIM]
        vx_ref[:, c0 + HEAD_DIM:c0 + VX] = ones


def _key_loader(ref, rows, head0):
    return lambda h: ref[rows, (head0 + h) * HEAD_DIM:(head0 + h + 1) * HEAD_DIM]


def _value_loader(ref, rows, head0):
    return lambda h: ref[rows, (head0 + h) * VX:(head0 + h + 1) * VX]


_KV_HEAD0 = {"a": 0, "b": A_KV, "c": A_KV + B_HEADS}


def _exp_weighted(scores, values, floor=None):
    m = functools.reduce(jnp.maximum, [jnp.max(s, axis=-1, keepdims=True) for s in scores])
    if floor is not None:
        m = jnp.maximum(m, floor)
    acc = functools.reduce(lambda a, b: a + b,
                           [_dot(jnp.exp2(s - m).astype(BF16), v) for s, v in zip(scores, values)])
    return acc, m


def _attend(q_ref, segs, o_ref, sink_ref, lam_init_ref, bl_ref, subln_ref, layer, local_mask=None):
    lam_init = lam_init_ref[layer]
    bl = bl_ref[...]
    s1 = jnp.sum(bl[0:1, :] * bl[1:2, :], axis=-1, keepdims=True)
    s2 = jnp.sum(bl[2:3, :] * bl[3:4, :], axis=-1, keepdims=True)
    lam = jnp.exp(s1) - jnp.exp(s2) + lam_init
    g_sub = subln_ref[...] * (1.0 - lam_init)
    lane = lax.broadcasted_iota(jnp.int32, (1, HEAD_DIM), 1)
    first_map = lane < B_QK_DIM

    def split(acc):
        return acc[:, :HEAD_DIM], acc[:, HEAD_DIM:]

    for i in range(A_HEADS):
        kv = i // (A_HEADS // A_KV)
        q = q_ref[:, i * HEAD_DIM:(i + 1) * HEAD_DIM]
        scores = [_dot_t(q, k(kv)) for k, _ in segs["a"]]
        if local_mask is not None:
            scores[-1] = jnp.where(local_mask, scores[-1], -jnp.inf)
        sink = sink_ref[layer, i] * LOG2_E
        acc, m = _exp_weighted(scores, [v(kv) for _, v in segs["a"]], floor=sink)
        num, den = split(acc)
        o = num / (den + jnp.exp2(sink - m))
        o_ref[:, i * HEAD_DIM:(i + 1) * HEAD_DIM] = o.astype(o_ref.dtype)

    for i in range(B_HEADS):
        c0 = (A_HEADS + i) * HEAD_DIM
        q = q_ref[:, c0:c0 + HEAD_DIM]
        q1 = jnp.where(first_map, q, jnp.zeros_like(q))
        q2 = jnp.where(first_map, jnp.zeros_like(q), q)
        values = [v(i) for _, v in segs["b"]]
        n1, d1 = split(_exp_weighted([_dot_t(q1, k(i)) for k, _ in segs["b"]], values)[0])
        n2, d2 = split(_exp_weighted([_dot_t(q2, k(i)) for k, _ in segs["b"]], values)[0])
        o = n1 / d1 - lam * (n2 / d2)
        o = _rms(o) * g_sub
        o_ref[:, c0:c0 + HEAD_DIM] = o.astype(o_ref.dtype)

    for i in range(C_HEADS):
        kv = i // (C_HEADS // C_KV)
        c0 = (A_HEADS + B_HEADS + i) * HEAD_DIM
        q = q_ref[:, c0:c0 + HEAD_DIM]
        acc, _ = _exp_weighted([_dot_t(q, k(kv)) for k, _ in segs["c"]], [v(kv) for _, v in segs["c"]])
        num, den = split(acc)
        o_ref[:, c0:c0 + HEAD_DIM] = (num / den).astype(o_ref.dtype)


def _ctx_attn_kernel(s_ref, q_ref, k_ref, v_ref, sink_ref, lam_init_ref, bl_ref, subln_ref, o_ref, vx_ref):
    _store_values_with_ones(v_ref[...], vx_ref, KV_HEADS)
    rows = slice(None)
    segs = {g: [(_key_loader(k_ref, rows, h0), _value_loader(vx_ref, rows, h0))]
            for g, h0 in _KV_HEAD0.items()}
    _attend(q_ref, segs, o_ref, sink_ref, lam_init_ref, bl_ref, subln_ref, s_ref[0])


def _attn_param_specs(n_grid):
    def layer_map(*args):
        return (args[n_grid][0], 0, 0)

    smem = pl.BlockSpec(memory_space=pltpu.SMEM)
    return [smem, smem,
            pl.BlockSpec((None, 4, B_QK_DIM), layer_map),
            pl.BlockSpec((None, 1, HEAD_DIM), layer_map)]


_K_COL_BLOCK = Q_WIDTH // KV_WIDTH
_V_COL_BLOCK = _K_COL_BLOCK + 1


def _ctx_attention(sel, qkv, a_sink, lam_init, b_lambda, b_subln, *, batch, seq):
    return pl.pallas_call(
        _ctx_attn_kernel,
        out_shape=jax.ShapeDtypeStruct((qkv.shape[0], Q_WIDTH), BF16),
        grid_spec=pltpu.PrefetchScalarGridSpec(
            num_scalar_prefetch=1,
            grid=(batch,),
            in_specs=[pl.BlockSpec((seq, Q_WIDTH), lambda b, s: (b, 0)),
                      pl.BlockSpec((seq, KV_WIDTH), lambda b, s: (b, _K_COL_BLOCK)),
                      pl.BlockSpec((seq, KV_WIDTH), lambda b, s: (b, _V_COL_BLOCK))] + _attn_param_specs(1),
            out_specs=pl.BlockSpec((seq, Q_WIDTH), lambda b, s: (b, 0)),
            scratch_shapes=[pltpu.VMEM((seq, KV_HEADS * VX), BF16)]),
        compiler_params=_params(("arbitrary",), 40 * 1024 * 1024),
        name="context_attention",
    )(sel, qkv, qkv, qkv, a_sink, lam_init, b_lambda, b_subln)


def _lat_attn_kernel(s_ref, o_in_ref, q_ref, k_ref, v_ref, cak_ref, cav_ref, cbk_ref, cbv_ref, cck_ref, ccv_ref,
                     sink_ref, lam_init_ref, bl_ref, subln_ref, o_ref, ck_ref, cvx_ref, vx_ref, *, lat_seq):
    del o_in_ref
    @pl.when(pl.program_id(1) == 0)
    def _():
        past = ck_ref.shape[0]
        h0 = 0
        for kref, vref in ((cak_ref, cav_ref), (cbk_ref, cbv_ref), (cck_ref, ccv_ref)):
            n_heads = kref.shape[0] // past
            for h in range(n_heads):
                head_rows = pl.ds(h, past, stride=n_heads)
                ck_ref[:, (h0 + h) * HEAD_DIM:(h0 + h + 1) * HEAD_DIM] = kref[head_rows, :].astype(BF16)
                _store_values_with_ones(vref[head_rows, :].astype(BF16), cvx_ref, 1, h0 + h)
            h0 += n_heads
        _store_values_with_ones(v_ref[...], vx_ref, KV_HEADS)

    tq = q_ref.shape[0]
    band = tq + 2 * WINDOW
    q0 = pl.program_id(1) * tq
    k0 = pl.multiple_of(jnp.clip(q0 - WINDOW, 0, lat_seq - band), WINDOW)
    qpos = q0 + lax.broadcasted_iota(jnp.int32, (tq, 1), 0)
    kpos = k0 + lax.broadcasted_iota(jnp.int32, (1, band), 1)
    local_mask = jnp.abs(qpos - kpos) <= WINDOW

    every = slice(None)
    lat_rows = {"a": pl.ds(k0, band), "b": every, "c": every}
    segs = {g: [(_key_loader(ck_ref, every, h0), _value_loader(cvx_ref, every, h0)),
                (_key_loader(k_ref, lat_rows[g], h0), _value_loader(vx_ref, lat_rows[g], h0))]
            for g, h0 in _KV_HEAD0.items()}
    _attend(q_ref, segs, o_ref, sink_ref, lam_init_ref, bl_ref, subln_ref, s_ref[0], local_mask=local_mask)


def _lat_attention(sel, o, qkv, caches, a_sink, lam_init, b_lambda, b_subln, *, n_ctx_rows, lat_batch, lat_seq):
    q_blk0 = n_ctx_rows // TQ_LAT
    kv_blk0 = n_ctx_rows // lat_seq
    n_q = lat_seq // TQ_LAT
    past = caches[0].shape[2] // _CACHE_HEADS[0]

    def cache_spec(a):
        return pl.BlockSpec((None, None) + a.shape[2:], lambda b, i, s: (b, s[0], 0, 0))

    return pl.pallas_call(
        functools.partial(_lat_attn_kernel, lat_seq=lat_seq),
        out_shape=jax.ShapeDtypeStruct(o.shape, o.dtype),
        grid_spec=pltpu.PrefetchScalarGridSpec(
            num_scalar_prefetch=1,
            grid=(lat_batch, n_q),
            in_specs=[pl.BlockSpec(memory_space=pl.ANY),
                      pl.BlockSpec((TQ_LAT, Q_WIDTH), lambda b, i, s: (q_blk0 + b * n_q + i, 0)),
                      pl.BlockSpec((lat_seq, KV_WIDTH), lambda b, i, s: (kv_blk0 + b, _K_COL_BLOCK)),
                      pl.BlockSpec((lat_seq, KV_WIDTH), lambda b, i, s: (kv_blk0 + b, _V_COL_BLOCK))]
                     + [cache_spec(a) for a in caches] + _attn_param_specs(2),
            out_specs=pl.BlockSpec((TQ_LAT, Q_WIDTH), lambda b, i, s: (q_blk0 + b * n_q + i, 0)),
            scratch_shapes=[pltpu.VMEM((past, KV_WIDTH), BF16),
                            pltpu.VMEM((past, KV_HEADS * VX), BF16),
                            pltpu.VMEM((lat_seq, KV_HEADS * VX), BF16)]),
        input_output_aliases={1: 0},
        compiler_params=_params(("arbitrary", "arbitrary"), V7X_VMEM_LIMIT_BYTES),
        name="latent_attention",
    )(sel, o, qkv, qkv, qkv, *caches, a_sink, lam_init, b_lambda, b_subln)


def _out_proj_kernel(s_ref, a_ref, x_ref, par_ref, w_ref, o_ref, rs_ref):
    n = pl.program_id(1)
    y = _dot(a_ref[...], w_ref[...])
    for slab in range(o_ref.shape[1] // TN_OUT):
        @pl.when(n == slab)
        def _():
            o_ref[:, slab * TN_OUT:(slab + 1) * TN_OUT] = y

    @pl.when(n == pl.num_programs(1) - 1)
    def _():
        _gated_residual_rows(x_ref, o_ref, o_ref, par_ref, rs_ref, 1.0)


def _out_proj(sel, a, x, params, w_out, *, n_ctx_rows, lat_rows):
    m, d = x.shape
    kdim = a.shape[1]
    cond = functools.partial(_cond_of_tile, tm=TM, n_ctx_rows=n_ctx_rows, lat_rows=lat_rows)
    return pl.pallas_call(
        _out_proj_kernel,
        out_shape=jax.ShapeDtypeStruct((m, d), F32),
        grid_spec=pltpu.PrefetchScalarGridSpec(
            num_scalar_prefetch=1,
            grid=(m // TM, d // TN_OUT),
            in_specs=[
                pl.BlockSpec((TM, kdim), lambda i, n, s: (i, 0)),
                pl.BlockSpec((TM, d), lambda i, n, s: (i, 0)),
                _sublayer_spec(d, cond, 2),
                pl.BlockSpec((None, kdim, TN_OUT), lambda i, n, s: (s[0], 0, n)),
            ],
            out_specs=pl.BlockSpec((TM, d), lambda i, n, s: (i, 0)),
            scratch_shapes=[pltpu.VMEM((TM, LANES), F32)]),
        compiler_params=_params(("arbitrary", "arbitrary"), V7X_VMEM_LIMIT_BYTES),
        name="out_projection",
    )(sel, a, x, params, w_out)


def _rope_tables(n_ctx_rows, lat_batch, lat_seq):
    t = np.arange(lat_seq)
    row = (t // GRID_W).astype(np.float32)
    col = (t % GRID_W).astype(np.float32)

    def tables(dim):
        nf = dim // 4
        inv = jnp.asarray(ROPE_THETA, F32) ** (-jnp.arange(nf, dtype=F32) / nf)
        ang = jnp.concatenate([row[:, None] * inv, col[:, None] * inv], axis=-1)
        cos, sin = jnp.cos(ang), jnp.sin(ang)
        cos_t = jnp.tile(jnp.concatenate([cos, cos], axis=-1), (1, LANES // dim))
        sin_t = jnp.tile(jnp.concatenate([-sin, sin], axis=-1), (1, LANES // dim))
        return (jnp.concatenate([jnp.ones((n_ctx_rows, LANES), F32)] + [cos_t] * lat_batch, axis=0),
                jnp.concatenate([jnp.zeros((n_ctx_rows, LANES), F32)] + [sin_t] * lat_batch, axis=0))

    return jnp.concatenate(tables(HEAD_DIM) + tables(B_QK_DIM), axis=1)


def kernel(x_prompt, x_sample, cache_a_k, cache_a_v, cache_b_k, cache_b_v, cache_c_k, cache_c_v,
           c, c_ctx, w_mod, b_mod, norm_pre, norm_post, ffn_gate, ffn_up, ffn_down,
           w_in, w_out, a_sink, b_lambda, b_subln, c_qnorm, c_knorm):
    batch, seq, d = x_prompt.shape
    lat_batch, lat_seq, _ = x_sample.shape
    depth = w_mod.shape[0]
    past = cache_a_k.shape[2]
    n_ctx_rows = batch * seq
    assert d == D_MODEL and depth == DEPTH and w_in.shape[-1] == IN_WIDTH
    assert n_ctx_rows % TM == 0 and lat_seq % TM == 0 and 1 + lat_batch <= N_COND
    assert n_ctx_rows % lat_seq == 0 and lat_seq % TQ_LAT == 0 and TM % seq == 0

    x = jnp.concatenate([x_prompt.reshape(n_ctx_rows, d), x_sample.reshape(lat_batch * lat_seq, d)], axis=0)
    cond = jnp.concatenate([c_ctx[None, :], c, jnp.zeros((N_COND - 1 - lat_batch, d), F32)], axis=0)
    mod = _modulation(cond, w_mod, b_mod).reshape(depth, N_COND, N_SUB, 3, d)
    params = _sublayer_params(mod, norm_pre, norm_post)
    rope = _rope_tables(n_ctx_rows, lat_batch, lat_seq)
    w_in = w_in.astype(BF16)
    w_out = w_out.astype(BF16)
    caches = [a.reshape(lat_batch, depth, -1, HEAD_DIM) for a in
              (cache_a_k, cache_a_v, cache_b_k, cache_b_v, cache_c_k, cache_c_v)]
    lam_init = jnp.asarray([0.8 - 0.6 * math.exp(-0.3 * l) for l in range(depth)], F32)
    subln = b_subln.reshape(depth, 1, HEAD_DIM)
    qk_gains = jnp.stack([c_qnorm, c_knorm], axis=1)
    geom = dict(n_ctx_rows=n_ctx_rows, lat_rows=lat_seq)

    new_caches = None
    for l in range(depth):
        sel = lambda sub, which=0: jnp.asarray([l, sub, which], jnp.int32)
        x = _ffn(sel(0, 0), x, params, ffn_gate, ffn_up, ffn_down, **geom)
        qkv, *new_caches = _in_proj(sel(1), x, params, w_in, rope, qk_gains, new_caches,
                                    batch=batch, seq=seq, **geom)
        o = _ctx_attention(sel(1), qkv, a_sink, lam_init, b_lambda, subln, batch=batch, seq=seq)
        o = _lat_attention(sel(1), o, qkv, caches, a_sink, lam_init, b_lambda, subln,
                           n_ctx_rows=n_ctx_rows, lat_batch=lat_batch, lat_seq=lat_seq)
        x = _out_proj(sel(1), o, x, params, w_out, **geom)
        x = _ffn(sel(2, 1), x, params, ffn_gate, ffn_up, ffn_down, **geom)

    new_kv = [a.reshape(batch, depth, seq, heads, HEAD_DIM) for a, heads in zip(new_caches, _CACHE_HEADS)]

    y_prompt = x[:n_ctx_rows].reshape(batch, seq, d)
    y_sample = x[n_ctx_rows:].reshape(lat_batch, lat_seq, d)
    return (y_prompt, y_sample) + tuple(new_kv)
```

```python
import functools
import math

import jax
import jax.numpy as jnp
import numpy as np
from jax import lax
from jax.experimental import pallas as pl
from jax.experimental.pallas import tpu as pltpu

D_MODEL = 2048
DEPTH = 4
GRID_W = 64
HEAD_DIM = 128
A_HEADS, A_KV = 6, 2
B_HEADS, B_QK_DIM = 4, 64
C_HEADS, C_KV = 6, 2
D_FF = 5632
WINDOW = 128
ROPE_THETA = 10000.0
EPS = 1e-6
FFN_RESID = 0.5
N_SUB = 3
N_COND = 8
Q_WIDTH = (A_HEADS + B_HEADS + C_HEADS) * HEAD_DIM
KV_HEADS = A_KV + B_HEADS + C_KV
KV_WIDTH = KV_HEADS * HEAD_DIM
IN_WIDTH = Q_WIDTH + 2 * KV_WIDTH
LOG2_E = math.log2(math.e)
_OFF = {}
_o = 0
for _name, _w in (("qa", A_HEADS), ("ka", A_KV), ("va", A_KV), ("qb", B_HEADS), ("kb", B_HEADS),
                  ("vb", B_HEADS), ("qc", C_HEADS), ("kc", C_KV), ("vc", C_KV)):
    _OFF[_name] = _o
    _o += _w * HEAD_DIM

V7X_VMEM_LIMIT_BYTES = 56 * 1024 * 1024
V7X_VMEM_LIMIT_FFN_BYTES = 58 * 1024 * 1024
LANES = 128

TM = 1024
TF = 512
TN_IN = 256
TN_OUT = 1024
TN_MOD = 1024
ROW_CHUNK = 32
TQ_LAT = 256

F32 = jnp.float32
BF16 = jnp.bfloat16


def _rms(x):
    return x * lax.rsqrt(jnp.mean(x * x, axis=-1, keepdims=True) + EPS)


def _dot(a, b):
    return jnp.dot(a, b, preferred_element_type=F32)


def _dot_t(a, b):
    return lax.dot_general(a, b, (((1,), (1,)), ((), ())), preferred_element_type=F32)


def _cond_of_tile(i, tm, n_ctx_rows, lat_rows):
    return jnp.maximum((i * tm - n_ctx_rows) // lat_rows + 1, 0)


def _params(semantics, vmem_bytes):
    return pltpu.CompilerParams(dimension_semantics=semantics, vmem_limit_bytes=vmem_bytes)


def _mod_kernel(cond_ref, w_ref, b_ref, o_ref):
    c = cond_ref[...]
    s = (c * jax.nn.sigmoid(c)).astype(BF16)
    o_ref[...] = _dot(s, w_ref[...].astype(BF16)) + b_ref[...]


def _modulation(cond, w_mod, b_mod):
    depth, d, n = w_mod.shape
    return pl.pallas_call(
        _mod_kernel,
        out_shape=jax.ShapeDtypeStruct((depth, N_COND, n), F32),
        grid=(depth, n // TN_MOD),
        in_specs=[
            pl.BlockSpec((N_COND, d), lambda l, j: (0, 0)),
            pl.BlockSpec((None, d, TN_MOD), lambda l, j: (l, 0, j)),
            pl.BlockSpec((None, 1, TN_MOD), lambda l, j: (l, 0, j)),
        ],
        out_specs=pl.BlockSpec((None, N_COND, TN_MOD), lambda l, j: (l, 0, j)),
        compiler_params=_params(("arbitrary", "arbitrary"), 40 * 1024 * 1024),
        name="adaln_modulation",
    )(cond, w_mod, b_mod.reshape(depth, 1, n))


def _row_chunks(n_rows, body, unroll):
    def step(r, carry):
        body(pl.ds(pl.multiple_of(r * ROW_CHUNK, ROW_CHUNK), ROW_CHUNK))
        return carry

    lax.fori_loop(0, n_rows // ROW_CHUNK, step, 0, unroll=unroll)


def _row_rsqrt_mean_square(src_ref, rs_ref):
    def body(rows):
        x = src_ref[rows, :]
        ms = jnp.mean(x * x, axis=-1, keepdims=True)
        rs_ref[rows, :] = jnp.broadcast_to(lax.rsqrt(ms + EPS), (ROW_CHUNK, LANES))

    _row_chunks(src_ref.shape[0], body, unroll=8)


_P_SHIFT, _P_SCALE, _P_GATE, _P_GPRE, _P_GPOST = range(5)
SUB_ROWS = 8


def _sublayer_params(mod, norm_pre, norm_post):
    depth, n_cond, n_sub, _, d = mod.shape
    gains = jnp.stack([norm_pre, norm_post], axis=2)
    gains = jnp.broadcast_to(gains[:, None], (depth, n_cond, n_sub, 2, d))
    pad = jnp.zeros((depth, n_cond, n_sub, SUB_ROWS - 5, d), F32)
    return jnp.concatenate([mod, gains, pad], axis=3)


def _modulated_norm_rows(x_ref, par_ref, h_ref, rs_ref):
    _row_rsqrt_mean_square(x_ref, rs_ref)
    shift = par_ref[_P_SHIFT:_P_SHIFT + 1, :]
    gain = par_ref[_P_GPRE:_P_GPRE + 1, :] * (1.0 + par_ref[_P_SCALE:_P_SCALE + 1, :])

    def body(rows):
        rs = rs_ref[rows, :]
        for t in range(x_ref.shape[1] // LANES):
            cols = slice(t * LANES, (t + 1) * LANES)
            h_ref[rows, cols] = (x_ref[rows, cols] * rs * gain[:, cols] + shift[:, cols]).astype(BF16)

    _row_chunks(x_ref.shape[0], body, unroll=2)


def _gated_residual_rows(x_ref, y_ref, o_ref, par_ref, rs_ref, weight):
    _row_rsqrt_mean_square(y_ref, rs_ref)
    gain = (weight * par_ref[_P_GATE:_P_GATE + 1, :]) * par_ref[_P_GPOST:_P_GPOST + 1, :]

    def body(rows):
        rs = rs_ref[rows, :]
        for t in range(x_ref.shape[1] // LANES):
            cols = slice(t * LANES, (t + 1) * LANES)
            o_ref[rows, cols] = x_ref[rows, cols] + y_ref[rows, cols] * rs * gain[:, cols]

    _row_chunks(x_ref.shape[0], body, unroll=2)


def _sublayer_spec(d, cond, n_grid):
    def par_map(*args):
        i, s = args[0], args[n_grid]
        return (s[0], cond(i), s[1], 0, 0)

    return pl.BlockSpec((None, None, None, SUB_ROWS, d), par_map)


def _ffn_kernel(s_ref, x_ref, par_ref, wg_ref, wu_ref, wd_ref, o_ref, h_ref, rs_ref):
    f = pl.program_id(1)

    @pl.when(f == 0)
    def _():
        _modulated_norm_rows(x_ref, par_ref, h_ref, rs_ref)
        o_ref[...] = jnp.zeros_like(o_ref)

    h = h_ref[...]
    g = _dot(h, wg_ref[...].astype(BF16))
    u = _dot(h, wu_ref[...].astype(BF16))
    a = (g * jax.nn.sigmoid(g) * u).astype(BF16)
    wd = wd_ref[...].astype(BF16)
    nchunk = 512
    for n in range(0, D_MODEL, nchunk):
        o_ref[:, n:n + nchunk] += _dot(a, wd[:, n:n + nchunk])

    @pl.when(f == pl.num_programs(1) - 1)
    def _():
        _gated_residual_rows(x_ref, o_ref, o_ref, par_ref, rs_ref, FFN_RESID)


def _ffn(sel, x, params, w_gate, w_up, w_down, *, n_ctx_rows, lat_rows):
    m, d = x.shape
    cond = functools.partial(_cond_of_tile, tm=TM, n_ctx_rows=n_ctx_rows, lat_rows=lat_rows)
    return pl.pallas_call(
        _ffn_kernel,
        out_shape=jax.ShapeDtypeStruct((m, d), F32),
        grid_spec=pltpu.PrefetchScalarGridSpec(
            num_scalar_prefetch=1,
            grid=(m // TM, D_FF // TF),
            in_specs=[
                pl.BlockSpec((TM, d), lambda i, f, s: (i, 0), pipeline_mode=pl.Buffered(1)),
                _sublayer_spec(d, cond, 2),
                pl.BlockSpec((None, None, d, TF), lambda i, f, s: (s[0], s[2], 0, f)),
                pl.BlockSpec((None, None, d, TF), lambda i, f, s: (s[0], s[2], 0, f)),
                pl.BlockSpec((None, None, TF, d), lambda i, f, s: (s[0], s[2], f, 0)),
            ],
            out_specs=pl.BlockSpec((TM, d), lambda i, f, s: (i, 0), pipeline_mode=pl.Buffered(1)),
            scratch_shapes=[pltpu.VMEM((TM, d), BF16), pltpu.VMEM((TM, LANES), F32)]),
        compiler_params=_params(("arbitrary", "arbitrary"), 52 * 1024 * 1024),
        name="ffn_sublayer",
    )(sel, x, params, w_gate, w_up, w_down)


_ROPE_NONE, _ROPE_HD, _ROPE_QK = 0, 1, 2
_NORM_NONE, _NORM_Q, _NORM_K = 0, 1, 2
_SCALE_ONE, _SCALE_HD, _SCALE_QK = 0, 1, 2


def _qkv_steps():
    blk = lambda name, j: _OFF[name] // TN_IN + j
    q0, k0, v0 = 0, Q_WIDTH // TN_IN, (Q_WIDTH + KV_WIDTH) // TN_IN
    kv = [(blk("ka", 0), k0 + 0, _ROPE_HD, _NORM_NONE), (blk("va", 0), v0 + 0, _ROPE_NONE, _NORM_NONE),
          (blk("kb", 0), k0 + 1, _ROPE_QK, _NORM_NONE), (blk("kb", 1), k0 + 2, _ROPE_QK, _NORM_NONE),
          (blk("vb", 0), v0 + 1, _ROPE_NONE, _NORM_NONE), (blk("vb", 1), v0 + 2, _ROPE_NONE, _NORM_NONE),
          (blk("kc", 0), k0 + 3, _ROPE_HD, _NORM_K), (blk("vc", 0), v0 + 3, _ROPE_NONE, _NORM_NONE)]
    rows = [(src, dst, rope, norm, _SCALE_ONE) for src, dst, rope, norm in kv]
    rows += ([(blk("qa", j), q0 + j, _ROPE_HD, _NORM_NONE, _SCALE_HD) for j in range(3)]
             + [(blk("qb", j), q0 + 3 + j, _ROPE_QK, _NORM_NONE, _SCALE_QK) for j in range(2)]
             + [(blk("qc", j), q0 + 5 + j, _ROPE_HD, _NORM_Q, _SCALE_HD) for j in range(3)])
    return np.asarray(rows, np.int32)


_QKV_STEPS = _qkv_steps()
_CACHE_HEADS = (A_KV, A_KV, B_HEADS, B_HEADS, C_KV, C_KV)
_KV_STEP_CACHE = ((0, 0), (1, 0), (2, 0), (2, 2), (3, 0), (3, 2), (4, 0), (5, 0))
_ST_SRC, _ST_DST, _ST_ROPE, _ST_NORM, _ST_SCALE = range(5)
N_QKV_STEPS = len(_QKV_STEPS)
QKV_SUB_ROWS = 256


def _in_proj_kernel(s_ref, st_ref, *refs, n_ctx_tiles, has_prev_cache):
    n_caches = len(_CACHE_HEADS)
    if has_prev_cache:
        refs = refs[n_caches:]
    x_ref, par_ref, w_ref, rope_ref, qkn_ref, qkv_ref = refs[:6]
    cache_refs = refs[6:6 + n_caches]
    h_ref, rs_ref, n_ref, pa_ref, pb_ref = refs[6 + n_caches:]
    rope_cols = {_ROPE_HD: (0, LANES), _ROPE_QK: (2 * LANES, 3 * LANES)}
    i, c = pl.program_id(0), pl.program_id(1)

    @pl.when(c == 0)
    def _():
        _modulated_norm_rows(x_ref, par_ref, h_ref, rs_ref)

    def project(p_ref):
        p_ref[...] = _dot(h_ref[...], w_ref[...])

    b = jnp.maximum(c - 1, 0)
    rope, norm, scale_kind = st_ref[b, _ST_ROPE], st_ref[b, _ST_NORM], st_ref[b, _ST_SCALE]

    def finish(p_ref, rope_kind, use_norm):
        gain = jnp.where(norm == _NORM_Q, qkn_ref[0:1, :], qkn_ref[1:2, :])
        scale = jnp.where(scale_kind == _SCALE_HD, HEAD_DIM ** -0.5 * LOG2_E,
                          jnp.where(scale_kind == _SCALE_QK, B_QK_DIM ** -0.5 * LOG2_E, 1.0))
        lane = lax.broadcasted_iota(jnp.int32, (1, LANES), 1)
        low_half = (lane % B_QK_DIM) < (B_QK_DIM // 2)
        cos0, sin0 = rope_cols.get(rope_kind, (0, 0))
        for r in range(p_ref.shape[0] // QKV_SUB_ROWS):
            rows = slice(r * QKV_SUB_ROWS, (r + 1) * QKV_SUB_ROWS)
            for hh in range(TN_IN // HEAD_DIM):
                cols = slice(hh * HEAD_DIM, (hh + 1) * HEAD_DIM)
                n = p_ref[rows, cols]
                if use_norm:
                    n = _rms(n) * gain
                n_ref[rows, cols] = n
                if rope_kind == _ROPE_HD:
                    partner = pltpu.roll(n, HEAD_DIM // 2, axis=1)
                elif rope_kind == _ROPE_QK:
                    partner = jnp.where(low_half, pltpu.roll(n, LANES - B_QK_DIM // 2, axis=1),
                                        pltpu.roll(n, B_QK_DIM // 2, axis=1))
                if rope_kind != _ROPE_NONE:
                    n = (n * rope_ref[rows, cos0:cos0 + LANES]
                         + partner * rope_ref[rows, sin0:sin0 + LANES])
                qkv_ref[rows, cols] = (n * scale).astype(BF16)

    recipes = sorted({(int(r[_ST_ROPE]), int(r[_ST_NORM]) != _NORM_NONE) for r in _QKV_STEPS})

    def finish_by_recipe(when, project_ref, finish_ref):
        for rope_kind, use_norm in recipes:
            @pl.when(when & (rope == rope_kind) & ((norm != _NORM_NONE) == use_norm))
            def _():
                if project_ref is not None:
                    project(project_ref)
                finish(finish_ref, rope_kind, use_norm)

    odd = c % 2 == 1
    inner = (c > 0) & (c < N_QKV_STEPS)

    @pl.when(c == 0)
    def _():
        project(pa_ref)

    finish_by_recipe(inner & odd, pb_ref, pa_ref)
    finish_by_recipe(inner & jnp.logical_not(odd), pa_ref, pb_ref)
    finish_by_recipe(c == N_QKV_STEPS, None, pa_ref if N_QKV_STEPS % 2 == 1 else pb_ref)

    for step, (cache, head0) in enumerate(_KV_STEP_CACHE):
        @pl.when((c == step + 1) & (i < n_ctx_tiles))
        def _():
            cache_ref, heads = cache_refs[cache], _CACHE_HEADS[cache]
            seq = cache_ref.shape[1] // heads
            for bb in range(cache_ref.shape[0]):
                for hh in range(TN_IN // HEAD_DIM):
                    cache_ref[bb, pl.ds(head0 + hh, seq, stride=heads), :] = (
                        n_ref[bb * seq:(bb + 1) * seq, hh * HEAD_DIM:(hh + 1) * HEAD_DIM])


def _in_proj(sel, x, params, w_in, rope, qk_gains, prev_cache, *, n_ctx_rows, lat_rows, batch, seq):
    m, d = x.shape
    depth = w_in.shape[0]
    n_ctx_tiles = n_ctx_rows // TM
    tile_batches = TM // seq
    cond = functools.partial(_cond_of_tile, tm=TM, n_ctx_rows=n_ctx_rows, lat_rows=lat_rows)
    projected = lambda c: jnp.minimum(c, N_QKV_STEPS - 1)
    finished = lambda c: jnp.maximum(c - 1, 0)

    def cache_spec(heads):
        return pl.BlockSpec((tile_batches, None, seq * heads, HEAD_DIM),
                            lambda i, c, s, st: (jnp.minimum(i, n_ctx_tiles - 1), s[0], 0, 0))

    has_prev = prev_cache is not None
    n_caches = len(_CACHE_HEADS)
    prev_specs = [pl.BlockSpec(memory_space=pl.ANY)] * n_caches if has_prev else []
    prev_args = list(prev_cache) if has_prev else []
    p_buffer = pltpu.VMEM((TM, TN_IN), F32)
    return pl.pallas_call(
        functools.partial(_in_proj_kernel, n_ctx_tiles=n_ctx_tiles, has_prev_cache=has_prev),
        out_shape=(jax.ShapeDtypeStruct((m, IN_WIDTH), BF16),)
                  + tuple(jax.ShapeDtypeStruct((batch, depth, seq * heads, HEAD_DIM), F32)
                          for heads in _CACHE_HEADS),
        grid_spec=pltpu.PrefetchScalarGridSpec(
            num_scalar_prefetch=2,
            grid=(m // TM, N_QKV_STEPS + 1),
            in_specs=prev_specs + [
                pl.BlockSpec((TM, d), lambda i, c, s, st: (i, 0)),
                _sublayer_spec(d, cond, 2),
                pl.BlockSpec((None, d, TN_IN), lambda i, c, s, st: (s[0], 0, st[projected(c), _ST_SRC])),
                pl.BlockSpec((TM, rope.shape[1]), lambda i, c, s, st: (i, 0)),
                pl.BlockSpec((None, 2, HEAD_DIM), lambda i, c, s, st: (s[0], 0, 0)),
            ],
            out_specs=(pl.BlockSpec((TM, TN_IN), lambda i, c, s, st: (i, st[finished(c), _ST_DST])),)
                      + tuple(cache_spec(heads) for heads in _CACHE_HEADS),
            scratch_shapes=[pltpu.VMEM((TM, d), BF16), pltpu.VMEM((TM, LANES), F32),
                            pltpu.VMEM((TM, TN_IN), F32), p_buffer, p_buffer]),
        input_output_aliases={2 + k: 1 + k for k in range(n_caches)} if has_prev else {},
        compiler_params=_params(("arbitrary", "arbitrary"), V7X_VMEM_LIMIT_BYTES),
        name="qkv_projection",
    )(sel, jnp.asarray(_QKV_STEPS), *prev_args, x, params, w_in, rope, qk_gains)


VX = 2 * HEAD_DIM


def _store_values_with_ones(v, vx_ref, n_heads, head0=0):
    ones = jnp.ones((v.shape[0], HEAD_DIM), BF16)
    for h in range(n_heads):
        c0 = (head0 + h) * VX
        vx_ref[:, c0:c0 + HEAD_DIM] = v[:, h * HEAD_DIM:(h + 1) * HEAD_DIM]
        vx_ref[:, c0 + HEAD_DIM:c0 + VX] = ones


def _key_loader(ref, rows, head0):
    return lambda h: ref[rows, (head0 + h) * HEAD_DIM:(head0 + h + 1) * HEAD_DIM]


def _value_loader(ref, rows, head0):
    return lambda h: ref[rows, (head0 + h) * VX:(head0 + h + 1) * VX]


_KV_HEAD0 = {"a": 0, "b": A_KV, "c": A_KV + B_HEADS}


def _exp_weighted(scores, values, floor=None):
    m = functools.reduce(jnp.maximum, [jnp.max(s, axis=-1, keepdims=True) for s in scores])
    if floor is not None:
        m = jnp.maximum(m, floor)
    acc = functools.reduce(lambda a, b: a + b,
                           [_dot(jnp.exp2(s - m).astype(BF16), v) for s, v in zip(scores, values)])
    return acc, m


def _attend(q_ref, segs, o_ref, sink_ref, lam_init_ref, bl_ref, subln_ref, layer, local_mask=None):
    lam_init = lam_init_ref[layer]
    bl = bl_ref[...]
    s1 = jnp.sum(bl[0:1, :] * bl[1:2, :], axis=-1, keepdims=True)
    s2 = jnp.sum(bl[2:3, :] * bl[3:4, :], axis=-1, keepdims=True)
    lam = jnp.exp(s1) - jnp.exp(s2) + lam_init
    g_sub = subln_ref[...] * (1.0 - lam_init)
    lane = lax.broadcasted_iota(jnp.int32, (1, HEAD_DIM), 1)
    first_map = lane < B_QK_DIM

    def split(acc):
        return acc[:, :HEAD_DIM], acc[:, HEAD_DIM:]

    for i in range(A_HEADS):
        kv = i // (A_HEADS // A_KV)
        q = q_ref[:, i * HEAD_DIM:(i + 1) * HEAD_DIM]
        scores = [_dot_t(q, k(kv)) for k, _ in segs["a"]]
        if local_mask is not None:
            scores[-1] = jnp.where(local_mask, scores[-1], -jnp.inf)
        sink = sink_ref[layer, i] * LOG2_E
        acc, m = _exp_weighted(scores, [v(kv) for _, v in segs["a"]], floor=sink)
        num, den = split(acc)
        o = num / (den + jnp.exp2(sink - m))
        o_ref[:, i * HEAD_DIM:(i + 1) * HEAD_DIM] = o.astype(o_ref.dtype)

    for i in range(B_HEADS):
        c0 = (A_HEADS + i) * HEAD_DIM
        q = q_ref[:, c0:c0 + HEAD_DIM]
        q1 = jnp.where(first_map, q, jnp.zeros_like(q))
        q2 = jnp.where(first_map, jnp.zeros_like(q), q)
        values = [v(i) for _, v in segs["b"]]
        n1, d1 = split(_exp_weighted([_dot_t(q1, k(i)) for k, _ in segs["b"]], values)[0])
        n2, d2 = split(_exp_weighted([_dot_t(q2, k(i)) for k, _ in segs["b"]], values)[0])
        o = n1 / d1 - lam * (n2 / d2)
        o = _rms(o) * g_sub
        o_ref[:, c0:c0 + HEAD_DIM] = o.astype(o_ref.dtype)

    for i in range(C_HEADS):
        kv = i // (C_HEADS // C_KV)
        c0 = (A_HEADS + B_HEADS + i) * HEAD_DIM
        q = q_ref[:, c0:c0 + HEAD_DIM]
        acc, _ = _exp_weighted([_dot_t(q, k(kv)) for k, _ in segs["c"]], [v(kv) for _, v in segs["c"]])
        num, den = split(acc)
        o_ref[:, c0:c0 + HEAD_DIM] = (num / den).astype(o_ref.dtype)


def _ctx_attn_kernel(s_ref, q_ref, k_ref, v_ref, sink_ref, lam_init_ref, bl_ref, subln_ref, o_ref, vx_ref):
    _store_values_with_ones(v_ref[...], vx_ref, KV_HEADS)
    rows = slice(None)
    segs = {g: [(_key_loader(k_ref, rows, h0), _value_loader(vx_ref, rows, h0))]
            for g, h0 in _KV_HEAD0.items()}
    _attend(q_ref, segs, o_ref, sink_ref, lam_init_ref, bl_ref, subln_ref, s_ref[0])


def _attn_param_specs(n_grid):
    def layer_map(*args):
        return (args[n_grid][0], 0, 0)

    smem = pl.BlockSpec(memory_space=pltpu.SMEM)
    return [smem, smem,
            pl.BlockSpec((None, 4, B_QK_DIM), layer_map),
            pl.BlockSpec((None, 1, HEAD_DIM), layer_map)]


_K_COL_BLOCK = Q_WIDTH // KV_WIDTH
_V_COL_BLOCK = _K_COL_BLOCK + 1


def _ctx_attention(sel, qkv, a_sink, lam_init, b_lambda, b_subln, *, batch, seq):
    return pl.pallas_call(
        _ctx_attn_kernel,
        out_shape=jax.ShapeDtypeStruct((qkv.shape[0], Q_WIDTH), BF16),
        grid_spec=pltpu.PrefetchScalarGridSpec(
            num_scalar_prefetch=1,
            grid=(batch,),
            in_specs=[pl.BlockSpec((seq, Q_WIDTH), lambda b, s: (b, 0)),
                      pl.BlockSpec((seq, KV_WIDTH), lambda b, s: (b, _K_COL_BLOCK)),
                      pl.BlockSpec((seq, KV_WIDTH), lambda b, s: (b, _V_COL_BLOCK))] + _attn_param_specs(1),
            out_specs=pl.BlockSpec((seq, Q_WIDTH), lambda b, s: (b, 0)),
            scratch_shapes=[pltpu.VMEM((seq, KV_HEADS * VX), BF16)]),
        compiler_params=_params(("arbitrary",), 40 * 1024 * 1024),
        name="context_attention",
    )(sel, qkv, qkv, qkv, a_sink, lam_init, b_lambda, b_subln)


def _lat_attn_kernel(s_ref, o_in_ref, q_ref, k_ref, v_ref, cak_ref, cav_ref, cbk_ref, cbv_ref, cck_ref, ccv_ref,
                     sink_ref, lam_init_ref, bl_ref, subln_ref, o_ref, ck_ref, cvx_ref, vx_ref, *, lat_seq):
    del o_in_ref
    @pl.when(pl.program_id(1) == 0)
    def _():
        past = ck_ref.shape[0]
        h0 = 0
        for kref, vref in ((cak_ref, cav_ref), (cbk_ref, cbv_ref), (cck_ref, ccv_ref)):
            n_heads = kref.shape[0] // past
            for h in range(n_heads):
                head_rows = pl.ds(h, past, stride=n_heads)
                ck_ref[:, (h0 + h) * HEAD_DIM:(h0 + h + 1) * HEAD_DIM] = kref[head_rows, :].astype(BF16)
                _store_values_with_ones(vref[head_rows, :].astype(BF16), cvx_ref, 1, h0 + h)
            h0 += n_heads
        _store_values_with_ones(v_ref[...], vx_ref, KV_HEADS)

    tq = q_ref.shape[0]
    band = tq + 2 * WINDOW
    q0 = pl.program_id(1) * tq
    k0 = pl.multiple_of(jnp.clip(q0 - WINDOW, 0, lat_seq - band), WINDOW)
    qpos = q0 + lax.broadcasted_iota(jnp.int32, (tq, 1), 0)
    kpos = k0 + lax.broadcasted_iota(jnp.int32, (1, band), 1)
    local_mask = jnp.abs(qpos - kpos) <= WINDOW

    every = slice(None)
    lat_rows = {"a": pl.ds(k0, band), "b": every, "c": every}
    segs = {g: [(_key_loader(ck_ref, every, h0), _value_loader(cvx_ref, every, h0)),
                (_key_loader(k_ref, lat_rows[g], h0), _value_loader(vx_ref, lat_rows[g], h0))]
            for g, h0 in _KV_HEAD0.items()}
    _attend(q_ref, segs, o_ref, sink_ref, lam_init_ref, bl_ref, subln_ref, s_ref[0], local_mask=local_mask)


def _lat_attention(sel, o, qkv, caches, a_sink, lam_init, b_lambda, b_subln, *, n_ctx_rows, lat_batch, lat_seq):
    q_blk0 = n_ctx_rows // TQ_LAT
    kv_blk0 = n_ctx_rows // lat_seq
    n_q = lat_seq // TQ_LAT
    past = caches[0].shape[2] // _CACHE_HEADS[0]

    def cache_spec(a):
        return pl.BlockSpec((None, None) + a.shape[2:], lambda b, i, s: (b, s[0], 0, 0))

    return pl.pallas_call(
        functools.partial(_lat_attn_kernel, lat_seq=lat_seq),
        out_shape=jax.ShapeDtypeStruct(o.shape, o.dtype),
        grid_spec=pltpu.PrefetchScalarGridSpec(
            num_scalar_prefetch=1,
            grid=(lat_batch, n_q),
            in_specs=[pl.BlockSpec(memory_space=pl.ANY),
                      pl.BlockSpec((TQ_LAT, Q_WIDTH), lambda b, i, s: (q_blk0 + b * n_q + i, 0)),
                      pl.BlockSpec((lat_seq, KV_WIDTH), lambda b, i, s: (kv_blk0 + b, _K_COL_BLOCK)),
                      pl.BlockSpec((lat_seq, KV_WIDTH), lambda b, i, s: (kv_blk0 + b, _V_COL_BLOCK))]
                     + [cache_spec(a) for a in caches] + _attn_param_specs(2),
            out_specs=pl.BlockSpec((TQ_LAT, Q_WIDTH), lambda b, i, s: (q_blk0 + b * n_q + i, 0)),
            scratch_shapes=[pltpu.VMEM((past, KV_WIDTH), BF16),
                            pltpu.VMEM((past, KV_HEADS * VX), BF16),
                            pltpu.VMEM((lat_seq, KV_HEADS * VX), BF16)]),
        input_output_aliases={1: 0},
        compiler_params=_params(("arbitrary", "arbitrary"), V7X_VMEM_LIMIT_BYTES),
        name="latent_attention",
    )(sel, o, qkv, qkv, qkv, *caches, a_sink, lam_init, b_lambda, b_subln)


def _out_proj_kernel(s_ref, a_ref, x_ref, par_ref, w_ref, o_ref, rs_ref):
    n = pl.program_id(1)
    y = _dot(a_ref[...], w_ref[...])
    for slab in range(o_ref.shape[1] // TN_OUT):
        @pl.when(n == slab)
        def _():
            o_ref[:, slab * TN_OUT:(slab + 1) * TN_OUT] = y

    @pl.when(n == pl.num_programs(1) - 1)
    def _():
        _gated_residual_rows(x_ref, o_ref, o_ref, par_ref, rs_ref, 1.0)


def _out_proj(sel, a, x, params, w_out, *, n_ctx_rows, lat_rows):
    m, d = x.shape
    kdim = a.shape[1]
    cond = functools.partial(_cond_of_tile, tm=TM, n_ctx_rows=n_ctx_rows, lat_rows=lat_rows)
    return pl.pallas_call(
        _out_proj_kernel,
        out_shape=jax.ShapeDtypeStruct((m, d), F32),
        grid_spec=pltpu.PrefetchScalarGridSpec(
            num_scalar_prefetch=1,
            grid=(m // TM, d // TN_OUT),
            in_specs=[
                pl.BlockSpec((TM, kdim), lambda i, n, s: (i, 0)),
                pl.BlockSpec((TM, d), lambda i, n, s: (i, 0)),
                _sublayer_spec(d, cond, 2),
                pl.BlockSpec((None, kdim, TN_OUT), lambda i, n, s: (s[0], 0, n)),
            ],
            out_specs=pl.BlockSpec((TM, d), lambda i, n, s: (i, 0)),
            scratch_shapes=[pltpu.VMEM((TM, LANES), F32)]),
        compiler_params=_params(("arbitrary", "arbitrary"), V7X_VMEM_LIMIT_BYTES),
        name="out_projection",
    )(sel, a, x, params, w_out)


def _rope_tables(n_ctx_rows, lat_batch, lat_seq):
    t = np.arange(lat_seq)
    row = (t // GRID_W).astype(np.float32)
    col = (t % GRID_W).astype(np.float32)

    def tables(dim):
        nf = dim // 4
        inv = jnp.asarray(ROPE_THETA, F32) ** (-jnp.arange(nf, dtype=F32) / nf)
        ang = jnp.concatenate([row[:, None] * inv, col[:, None] * inv], axis=-1)
        cos, sin = jnp.cos(ang), jnp.sin(ang)
        cos_t = jnp.tile(jnp.concatenate([cos, cos], axis=-1), (1, LANES // dim))
        sin_t = jnp.tile(jnp.concatenate([-sin, sin], axis=-1), (1, LANES // dim))
        return (jnp.concatenate([jnp.ones((n_ctx_rows, LANES), F32)] + [cos_t] * lat_batch, axis=0),
                jnp.concatenate([jnp.zeros((n_ctx_rows, LANES), F32)] + [sin_t] * lat_batch, axis=0))

    return jnp.concatenate(tables(HEAD_DIM) + tables(B_QK_DIM), axis=1)


def kernel(x_prompt, x_sample, cache_a_k, cache_a_v, cache_b_k, cache_b_v, cache_c_k, cache_c_v,
           c, c_ctx, w_mod, b_mod, norm_pre, norm_post, ffn_gate, ffn_up, ffn_down,
           w_in, w_out, a_sink, b_lambda, b_subln, c_qnorm, c_knorm):
    batch, seq, d = x_prompt.shape
    lat_batch, lat_seq, _ = x_sample.shape
    depth = w_mod.shape[0]
    past = cache_a_k.shape[2]
    n_ctx_rows = batch * seq
    assert d == D_MODEL and depth == DEPTH and w_in.shape[-1] == IN_WIDTH
    assert n_ctx_rows % TM == 0 and lat_seq % TM == 0 and 1 + lat_batch <= N_COND
    assert n_ctx_rows % lat_seq == 0 and lat_seq % TQ_LAT == 0 and TM % seq == 0

    x = jnp.concatenate([x_prompt.reshape(n_ctx_rows, d), x_sample.reshape(lat_batch * lat_seq, d)], axis=0)
    cond = jnp.concatenate([c_ctx[None, :], c, jnp.zeros((N_COND - 1 - lat_batch, d), F32)], axis=0)
    mod = _modulation(cond, w_mod, b_mod).reshape(depth, N_COND, N_SUB, 3, d)
    params = _sublayer_params(mod, norm_pre, norm_post)
    rope = _rope_tables(n_ctx_rows, lat_batch, lat_seq)
    w_in = w_in.astype(BF16)
    w_out = w_out.astype(BF16)
    caches = [a.reshape(lat_batch, depth, -1, HEAD_DIM) for a in
              (cache_a_k, cache_a_v, cache_b_k, cache_b_v, cache_c_k, cache_c_v)]
    lam_init = jnp.asarray([0.8 - 0.6 * math.exp(-0.3 * l) for l in range(depth)], F32)
    subln = b_subln.reshape(depth, 1, HEAD_DIM)
    qk_gains = jnp.stack([c_qnorm, c_knorm], axis=1)
    geom = dict(n_ctx_rows=n_ctx_rows, lat_rows=lat_seq)

    new_caches = None
    for l in range(depth):
        sel = lambda sub, which=0: jnp.asarray([l, sub, which], jnp.int32)
        x = _ffn(sel(0, 0), x, params, ffn_gate, ffn_up, ffn_down, **geom)
        qkv, *new_caches = _in_proj(sel(1), x, params, w_in, rope, qk_gains, new_caches,
                                    batch=batch, seq=seq, **geom)
        o = _ctx_attention(sel(1), qkv, a_sink, lam_init, b_lambda, subln, batch=batch, seq=seq)
        o = _lat_attention(sel(1), o, qkv, caches, a_sink, lam_init, b_lambda, subln,
                           n_ctx_rows=n_ctx_rows, lat_batch=lat_batch, lat_seq=lat_seq)
        x = _out_proj(sel(1), o, x, params, w_out, **geom)
        x = _ffn(sel(2, 1), x, params, ffn_gate, ffn_up, ffn_down, **geom)

    new_kv = [a.reshape(batch, depth, seq, heads, HEAD_DIM) for a, heads in zip(new_caches, _CACHE_HEADS)]

    y_prompt = x[:n_ctx_rows].reshape(batch, seq, d)
    y_sample = x[n_ctx_rows:].reshape(lat_batch, lat_seq, d)
    return (y_prompt, y_sample) + tuple(new_kv)
```

```python
import functools
import math

import jax
import jax.numpy as jnp
import numpy as np
from jax import lax
from jax.experimental import pallas as pl
from jax.experimental.pallas import tpu as pltpu

D_MODEL = 2048
DEPTH = 4
GRID_W = 64
HEAD_DIM = 128
A_HEADS, A_KV = 6, 2
B_HEADS, B_QK_DIM = 4, 64
C_HEADS, C_KV = 6, 2
D_FF = 5632
WINDOW = 128
ROPE_THETA = 10000.0
EPS = 1e-6
FFN_RESID = 0.5
N_SUB = 3
N_COND = 8
Q_WIDTH = (A_HEADS + B_HEADS + C_HEADS) * HEAD_DIM
KV_HEADS = A_KV + B_HEADS + C_KV
KV_WIDTH = KV_HEADS * HEAD_DIM
IN_WIDTH = Q_WIDTH + 2 * KV_WIDTH
LOG2_E = math.log2(math.e)
_OFF = {}
_o = 0
for _name, _w in (("qa", A_HEADS), ("ka", A_KV), ("va", A_KV), ("qb", B_HEADS), ("kb", B_HEADS),
                  ("vb", B_HEADS), ("qc", C_HEADS), ("kc", C_KV), ("vc", C_KV)):
    _OFF[_name] = _o
    _o += _w * HEAD_DIM

V7X_VMEM_LIMIT_BYTES = 56 * 1024 * 1024
LANES = 128

TM = 1024
TF = 256
TN_IN = 256
TN_OUT = 1024
TN_MOD = 1024
ROW_CHUNK = 32
TQ_LAT = 256

F32 = jnp.float32
BF16 = jnp.bfloat16


def _rms(x):
    return x * lax.rsqrt(jnp.mean(x * x, axis=-1, keepdims=True) + EPS)


def _dot(a, b):
    return jnp.dot(a, b, preferred_element_type=F32)


def _dot_t(a, b):
    return lax.dot_general(a, b, (((1,), (1,)), ((), ())), preferred_element_type=F32)


def _cond_of_tile(i, tm, n_ctx_rows, lat_rows):
    return jnp.maximum((i * tm - n_ctx_rows) // lat_rows + 1, 0)


def _params(semantics, vmem_bytes):
    return pltpu.CompilerParams(dimension_semantics=semantics, vmem_limit_bytes=vmem_bytes)


def _mod_kernel(cond_ref, w_ref, b_ref, o_ref):
    c = cond_ref[...]
    s = (c * jax.nn.sigmoid(c)).astype(BF16)
    o_ref[...] = _dot(s, w_ref[...].astype(BF16)) + b_ref[...]


def _modulation(cond, w_mod, b_mod):
    depth, d, n = w_mod.shape
    return pl.pallas_call(
        _mod_kernel,
        out_shape=jax.ShapeDtypeStruct((depth, N_COND, n), F32),
        grid=(depth, n // TN_MOD),
        in_specs=[
            pl.BlockSpec((N_COND, d), lambda l, j: (0, 0)),
            pl.BlockSpec((None, d, TN_MOD), lambda l, j: (l, 0, j)),
            pl.BlockSpec((None, 1, TN_MOD), lambda l, j: (l, 0, j)),
        ],
        out_specs=pl.BlockSpec((None, N_COND, TN_MOD), lambda l, j: (l, 0, j)),
        compiler_params=_params(("arbitrary", "arbitrary"), 40 * 1024 * 1024),
        name="adaln_modulation",
    )(cond, w_mod, b_mod.reshape(depth, 1, n))


def _row_chunks(n_rows, body, unroll):
    def step(r, carry):
        body(pl.ds(pl.multiple_of(r * ROW_CHUNK, ROW_CHUNK), ROW_CHUNK))
        return carry

    lax.fori_loop(0, n_rows // ROW_CHUNK, step, 0, unroll=unroll)


def _row_rsqrt_mean_square(src_parts, rs_ref):
    width = sum(p.shape[1] for p in src_parts)

    def body(rows):
        ss = None
        for part in src_parts:
            x = part[rows, :]
            s = jnp.sum(x * x, axis=-1, keepdims=True)
            ss = s if ss is None else ss + s
        rs_ref[rows, :] = jnp.broadcast_to(lax.rsqrt(ss * (1.0 / width) + EPS), (ROW_CHUNK, LANES))

    _row_chunks(src_parts[0].shape[0], body, unroll=8)


_P_SHIFT, _P_SCALE, _P_GATE, _P_GPRE, _P_GPOST = range(5)
SUB_ROWS = 8


def _sublayer_params(mod, norm_pre, norm_post):
    depth, n_cond, n_sub, _, d = mod.shape
    gains = jnp.stack([norm_pre, norm_post], axis=2)
    gains = jnp.broadcast_to(gains[:, None], (depth, n_cond, n_sub, 2, d))
    pad = jnp.zeros((depth, n_cond, n_sub, SUB_ROWS - 5, d), F32)
    return jnp.concatenate([mod, gains, pad], axis=3)


def _modulated_norm_rows(x_parts, par_ref, h_ref, rs_ref, zero_ref=None):
    _row_rsqrt_mean_square(x_parts, rs_ref)
    shift = par_ref[_P_SHIFT:_P_SHIFT + 1, :]
    gain = par_ref[_P_GPRE:_P_GPRE + 1, :] * (1.0 + par_ref[_P_SCALE:_P_SCALE + 1, :])
    zeros = jnp.zeros((ROW_CHUNK, LANES), F32)

    def body(rows):
        rs = rs_ref[rows, :]
        c0 = 0
        for part in x_parts:
            for t in range(part.shape[1] // LANES):
                src = slice(t * LANES, (t + 1) * LANES)
                cols = slice(c0 + t * LANES, c0 + (t + 1) * LANES)
                h_ref[rows, cols] = (part[rows, src] * rs * gain[:, cols] + shift[:, cols]).astype(BF16)
                if zero_ref is not None:
                    zero_ref[rows, cols] = zeros
            c0 += part.shape[1]

    _row_chunks(x_parts[0].shape[0], body, unroll=2)


def _gated_residual_rows(x_ref, y_ref, o_ref, par_ref, rs_ref, weight):
    _row_rsqrt_mean_square([y_ref], rs_ref)
    gain = (weight * par_ref[_P_GATE:_P_GATE + 1, :]) * par_ref[_P_GPOST:_P_GPOST + 1, :]

    def body(rows):
        rs = rs_ref[rows, :]
        for t in range(x_ref.shape[1] // LANES):
            cols = slice(t * LANES, (t + 1) * LANES)
            o_ref[rows, cols] = x_ref[rows, cols] + y_ref[rows, cols] * rs * gain[:, cols]

    _row_chunks(x_ref.shape[0], body, unroll=2)


def _sublayer_spec(d, cond, n_grid):
    def par_map(*args):
        i, s = args[0], args[n_grid]
        return (s[0], cond(i), s[1], 0, 0)

    return pl.BlockSpec((None, None, None, SUB_ROWS, d), par_map)


def _ffn_kernel(s_ref, x_ref, par_ref, wg_ref, wu_ref, wd_ref, o_ref, h_ref, rs_ref):
    f = pl.program_id(1)

    @pl.when(f == 0)
    def _():
        _modulated_norm_rows([x_ref], par_ref, h_ref, rs_ref, zero_ref=o_ref)

    h = h_ref[...]
    g = _dot(h, wg_ref[...].astype(BF16))
    u = _dot(h, wu_ref[...].astype(BF16))
    a = (g * jax.nn.sigmoid(g) * u).astype(BF16)
    wd = wd_ref[...].astype(BF16)
    nchunk = 512
    for n in range(0, D_MODEL, nchunk):
        o_ref[:, n:n + nchunk] += _dot(a, wd[:, n:n + nchunk])

    @pl.when(f == pl.num_programs(1) - 1)
    def _():
        _gated_residual_rows(x_ref, o_ref, o_ref, par_ref, rs_ref, FFN_RESID)


def _ffn(sel, x, params, w_gate, w_up, w_down, *, n_ctx_rows, lat_rows):
    m, d = x.shape
    cond = functools.partial(_cond_of_tile, tm=TM, n_ctx_rows=n_ctx_rows, lat_rows=lat_rows)
    return pl.pallas_call(
        _ffn_kernel,
        out_shape=jax.ShapeDtypeStruct((m, d), F32),
        grid_spec=pltpu.PrefetchScalarGridSpec(
            num_scalar_prefetch=1,
            grid=(m // TM, D_FF // TF),
            in_specs=[
                pl.BlockSpec((TM, d), lambda i, f, s: (i, 0)),
                _sublayer_spec(d, cond, 2),
                pl.BlockSpec((None, None, d, TF), lambda i, f, s: (s[0], s[2], 0, f)),
                pl.BlockSpec((None, None, d, TF), lambda i, f, s: (s[0], s[2], 0, f)),
                pl.BlockSpec((None, None, TF, d), lambda i, f, s: (s[0], s[2], f, 0)),
            ],
            out_specs=pl.BlockSpec((TM, d), lambda i, f, s: (i, 0)),
            scratch_shapes=[pltpu.VMEM((TM, d), BF16), pltpu.VMEM((TM, LANES), F32)]),
        compiler_params=_params(("arbitrary", "arbitrary"), V7X_VMEM_LIMIT_BYTES),
        name="ffn_sublayer",
    )(sel, x, params, w_gate, w_up, w_down)


_ROPE_NONE, _ROPE_HD, _ROPE_QK = 0, 1, 2
_NORM_NONE, _NORM_Q, _NORM_K = 0, 1, 2
_SCALE_ONE, _SCALE_HD, _SCALE_QK = 0, 1, 2


def _qkv_steps():
    blk = lambda name, j: _OFF[name] // TN_IN + j
    q0, k0, v0 = 0, Q_WIDTH // TN_IN, (Q_WIDTH + KV_WIDTH) // TN_IN
    kv = [(blk("ka", 0), k0 + 0, _ROPE_HD, _NORM_NONE), (blk("va", 0), v0 + 0, _ROPE_NONE, _NORM_NONE),
          (blk("kb", 0), k0 + 1, _ROPE_QK, _NORM_NONE), (blk("kb", 1), k0 + 2, _ROPE_QK, _NORM_NONE),
          (blk("vb", 0), v0 + 1, _ROPE_NONE, _NORM_NONE), (blk("vb", 1), v0 + 2, _ROPE_NONE, _NORM_NONE),
          (blk("kc", 0), k0 + 3, _ROPE_HD, _NORM_K), (blk("vc", 0), v0 + 3, _ROPE_NONE, _NORM_NONE)]
    rows = [(src, dst, rope, norm, _SCALE_ONE) for src, dst, rope, norm in kv]
    rows += ([(blk("qa", j), q0 + j, _ROPE_HD, _NORM_NONE, _SCALE_HD) for j in range(3)]
             + [(blk("qb", j), q0 + 3 + j, _ROPE_QK, _NORM_NONE, _SCALE_QK) for j in range(2)]
             + [(blk("qc", j), q0 + 5 + j, _ROPE_HD, _NORM_Q, _SCALE_HD) for j in range(3)])
    return np.asarray(rows, np.int32)


_QKV_STEPS = _qkv_steps()
_CACHE_HEADS = (A_KV, A_KV, B_HEADS, B_HEADS, C_KV, C_KV)
_KV_STEP_CACHE = ((0, 0), (1, 0), (2, 0), (2, 2), (3, 0), (3, 2), (4, 0), (5, 0))
_ST_SRC, _ST_DST, _ST_ROPE, _ST_NORM, _ST_SCALE = range(5)
N_QKV_STEPS = len(_QKV_STEPS)
QKV_SUB_ROWS = 256


def _in_proj_kernel(s_ref, st_ref, *refs, n_ctx_tiles, has_prev_cache):
    n_caches = len(_CACHE_HEADS)
    if has_prev_cache:
        refs = refs[n_caches:]
    xa_ref, xb_ref, par_ref, w_ref, rope_ref, qkn_ref, qkv_ref = refs[:7]
    cache_refs = refs[7:7 + n_caches]
    h_ref, rs_ref, n_ref, pa_ref, pb_ref = refs[7 + n_caches:]
    rope_cols = {_ROPE_HD: (0, LANES), _ROPE_QK: (2 * LANES, 3 * LANES)}
    i, c = pl.program_id(0), pl.program_id(1)

    @pl.when(c == 0)
    def _():
        _modulated_norm_rows([xa_ref, xb_ref], par_ref, h_ref, rs_ref)

    def project(p_ref):
        p_ref[...] = _dot(h_ref[...], w_ref[...])

    b = jnp.maximum(c - 1, 0)
    rope, norm, scale_kind = st_ref[b, _ST_ROPE], st_ref[b, _ST_NORM], st_ref[b, _ST_SCALE]

    def finish(p_ref, rope_kind, use_norm):
        gain = jnp.where(norm == _NORM_Q, qkn_ref[0:1, :], qkn_ref[1:2, :])
        scale = jnp.where(scale_kind == _SCALE_HD, HEAD_DIM ** -0.5 * LOG2_E,
                          jnp.where(scale_kind == _SCALE_QK, B_QK_DIM ** -0.5 * LOG2_E, 1.0))
        lane = lax.broadcasted_iota(jnp.int32, (1, LANES), 1)
        low_half = (lane % B_QK_DIM) < (B_QK_DIM // 2)
        cos0, sin0 = rope_cols.get(rope_kind, (0, 0))
        for r in range(p_ref.shape[0] // QKV_SUB_ROWS):
            rows = slice(r * QKV_SUB_ROWS, (r + 1) * QKV_SUB_ROWS)
            for hh in range(TN_IN // HEAD_DIM):
                cols = slice(hh * HEAD_DIM, (hh + 1) * HEAD_DIM)
                n = p_ref[rows, cols]
                if use_norm:
                    n = _rms(n) * gain
                n_ref[rows, cols] = n
                if rope_kind == _ROPE_HD:
                    partner = pltpu.roll(n, HEAD_DIM // 2, axis=1)
                elif rope_kind == _ROPE_QK:
                    partner = jnp.where(low_half, pltpu.roll(n, LANES - B_QK_DIM // 2, axis=1),
                                        pltpu.roll(n, B_QK_DIM // 2, axis=1))
                if rope_kind != _ROPE_NONE:
                    n = (n * rope_ref[rows, cos0:cos0 + LANES]
                         + partner * rope_ref[rows, sin0:sin0 + LANES])
                qkv_ref[rows, cols] = (n * scale).astype(BF16)

    recipes = sorted({(int(r[_ST_ROPE]), int(r[_ST_NORM]) != _NORM_NONE) for r in _QKV_STEPS})

    def finish_by_recipe(when, project_ref, finish_ref):
        for rope_kind, use_norm in recipes:
            @pl.when(when & (rope == rope_kind) & ((norm != _NORM_NONE) == use_norm))
            def _():
                if project_ref is not None:
                    project(project_ref)
                finish(finish_ref, rope_kind, use_norm)

    odd = c % 2 == 1
    inner = (c > 0) & (c < N_QKV_STEPS)

    @pl.when(c == 0)
    def _():
        project(pa_ref)

    finish_by_recipe(inner & odd, pb_ref, pa_ref)
    finish_by_recipe(inner & jnp.logical_not(odd), pa_ref, pb_ref)
    finish_by_recipe(c == N_QKV_STEPS, None, pa_ref if N_QKV_STEPS % 2 == 1 else pb_ref)

    for step, (cache, head0) in enumerate(_KV_STEP_CACHE):
        @pl.when((c == step + 1) & (i < n_ctx_tiles))
        def _():
            cache_ref, heads = cache_refs[cache], _CACHE_HEADS[cache]
            seq = cache_ref.shape[1] // heads
            for bb in range(cache_ref.shape[0]):
                for hh in range(TN_IN // HEAD_DIM):
                    cache_ref[bb, pl.ds(head0 + hh, seq, stride=heads), :] = (
                        n_ref[bb * seq:(bb + 1) * seq, hh * HEAD_DIM:(hh + 1) * HEAD_DIM])


def _in_proj(sel, x, params, w_in, rope, qk_gains, prev_cache, *, n_ctx_rows, lat_rows, batch, seq):
    m, d = x.shape
    depth = w_in.shape[0]
    n_ctx_tiles = n_ctx_rows // TM
    tile_batches = TM // seq
    cond = functools.partial(_cond_of_tile, tm=TM, n_ctx_rows=n_ctx_rows, lat_rows=lat_rows)
    projected = lambda c: jnp.minimum(c, N_QKV_STEPS - 1)
    finished = lambda c: jnp.maximum(c - 1, 0)

    def row_half(half):
        first_step = N_QKV_STEPS - 1 + half

        def index_map(i, c, s, st):
            return (jnp.minimum(jnp.where(c >= first_step, i + 1, i), m // TM - 1), half)

        return pl.BlockSpec((TM, d // 2), index_map)

    def cache_spec(heads):
        return pl.BlockSpec((tile_batches, None, seq * heads, HEAD_DIM),
                            lambda i, c, s, st: (jnp.minimum(i, n_ctx_tiles - 1), s[0], 0, 0))

    has_prev = prev_cache is not None
    n_caches = len(_CACHE_HEADS)
    prev_specs = [pl.BlockSpec(memory_space=pl.ANY)] * n_caches if has_prev else []
    prev_args = list(prev_cache) if has_prev else []
    p_buffer = pltpu.VMEM((TM, TN_IN), F32)
    return pl.pallas_call(
        functools.partial(_in_proj_kernel, n_ctx_tiles=n_ctx_tiles, has_prev_cache=has_prev),
        out_shape=(jax.ShapeDtypeStruct((m, IN_WIDTH), BF16),)
                  + tuple(jax.ShapeDtypeStruct((batch, depth, seq * heads, HEAD_DIM), F32)
                          for heads in _CACHE_HEADS),
        grid_spec=pltpu.PrefetchScalarGridSpec(
            num_scalar_prefetch=2,
            grid=(m // TM, N_QKV_STEPS + 1),
            in_specs=prev_specs + [
                row_half(0), row_half(1),
                _sublayer_spec(d, cond, 2),
                pl.BlockSpec((None, d, TN_IN), lambda i, c, s, st: (s[0], 0, st[projected(c), _ST_SRC])),
                pl.BlockSpec((TM, rope.shape[1]), lambda i, c, s, st: (i, 0)),
                pl.BlockSpec((None, 2, HEAD_DIM), lambda i, c, s, st: (s[0], 0, 0)),
            ],
            out_specs=(pl.BlockSpec((TM, TN_IN), lambda i, c, s, st: (i, st[finished(c), _ST_DST])),)
                      + tuple(cache_spec(heads) for heads in _CACHE_HEADS),
            scratch_shapes=[pltpu.VMEM((TM, d), BF16), pltpu.VMEM((TM, LANES), F32),
                            pltpu.VMEM((TM, TN_IN), F32), p_buffer, p_buffer]),
        input_output_aliases={2 + k: 1 + k for k in range(n_caches)} if has_prev else {},
        compiler_params=_params(("arbitrary", "arbitrary"), V7X_VMEM_LIMIT_BYTES),
        name="qkv_projection",
    )(sel, jnp.asarray(_QKV_STEPS), *prev_args, x, x, params, w_in, rope, qk_gains)


VX = 2 * HEAD_DIM


def _store_values_with_ones(v, vx_ref, n_heads, head0=0):
    ones = jnp.ones((v.shape[0], HEAD_DIM), BF16)
    for h in range(n_heads):
        c0 = (head0 + h) * VX
        vx_ref[:, c0:c0 + HEAD_DIM] = v[:, h * HEAD_DIM:(h + 1) * HEAD_DIM]
        vx_ref[:, c0 + HEAD_DIM:c0 + VX] = ones


def _key_loader(ref, rows, head0):
    return lambda h: ref[rows, (head0 + h) * HEAD_DIM:(head0 + h + 1) * HEAD_DIM]


def _value_loader(ref, rows, head0):
    return lambda h: ref[rows, (head0 + h) * VX:(head0 + h + 1) * VX]


_KV_HEAD0 = {"a": 0, "b": A_KV, "c": A_KV + B_HEADS}


def _exp_weighted(scores, values, floor=None):
    m = functools.reduce(jnp.maximum, [jnp.max(s, axis=-1, keepdims=True) for s in scores])
    if floor is not None:
        m = jnp.maximum(m, floor)
    acc = functools.reduce(lambda a, b: a + b,
                           [_dot(jnp.exp2(s - m).astype(BF16), v) for s, v in zip(scores, values)])
    return acc, m


def _attend(q_ref, segs, o_ref, sink_ref, lam_init_ref, bl_ref, subln_ref, layer, local_mask=None):
    lam_init = lam_init_ref[layer]
    bl = bl_ref[...]
    s1 = jnp.sum(bl[0:1, :] * bl[1:2, :], axis=-1, keepdims=True)
    s2 = jnp.sum(bl[2:3, :] * bl[3:4, :], axis=-1, keepdims=True)
    lam = jnp.exp(s1) - jnp.exp(s2) + lam_init
    g_sub = subln_ref[...] * (1.0 - lam_init)
    lane = lax.broadcasted_iota(jnp.int32, (1, HEAD_DIM), 1)
    first_map = lane < B_QK_DIM

    def split(acc):
        return acc[:, :HEAD_DIM], acc[:, HEAD_DIM:]

    for i in range(A_HEADS):
        kv = i // (A_HEADS // A_KV)
        q = q_ref[:, i * HEAD_DIM:(i + 1) * HEAD_DIM]
        scores = [_dot_t(q, k(kv)) for k, _ in segs["a"]]
        if local_mask is not None:
            scores[-1] = jnp.where(local_mask, scores[-1], -jnp.inf)
        sink = sink_ref[layer, i] * LOG2_E
        acc, m = _exp_weighted(scores, [v(kv) for _, v in segs["a"]], floor=sink)
        num, den = split(acc)
        o = num / (den + jnp.exp2(sink - m))
        o_ref[:, i * HEAD_DIM:(i + 1) * HEAD_DIM] = o.astype(o_ref.dtype)

    for i in range(B_HEADS):
        c0 = (A_HEADS + i) * HEAD_DIM
        q = q_ref[:, c0:c0 + HEAD_DIM]
        q1 = jnp.where(first_map, q, jnp.zeros_like(q))
        q2 = jnp.where(first_map, jnp.zeros_like(q), q)
        values = [v(i) for _, v in segs["b"]]
        n1, d1 = split(_exp_weighted([_dot_t(q1, k(i)) for k, _ in segs["b"]], values)[0])
        n2, d2 = split(_exp_weighted([_dot_t(q2, k(i)) for k, _ in segs["b"]], values)[0])
        o = n1 / d1 - lam * (n2 / d2)
        o = _rms(o) * g_sub
        o_ref[:, c0:c0 + HEAD_DIM] = o.astype(o_ref.dtype)

    for i in range(C_HEADS):
        kv = i // (C_HEADS // C_KV)
        c0 = (A_HEADS + B_HEADS + i) * HEAD_DIM
        q = q_ref[:, c0:c0 + HEAD_DIM]
        acc, _ = _exp_weighted([_dot_t(q, k(kv)) for k, _ in segs["c"]], [v(kv) for _, v in segs["c"]])
        num, den = split(acc)
        o_ref[:, c0:c0 + HEAD_DIM] = (num / den).astype(o_ref.dtype)


def _ctx_attn_kernel(s_ref, q_ref, k_ref, v_ref, sink_ref, lam_init_ref, bl_ref, subln_ref, o_ref, vx_ref):
    _store_values_with_ones(v_ref[...], vx_ref, KV_HEADS)
    rows = slice(None)
    segs = {g: [(_key_loader(k_ref, rows, h0), _value_loader(vx_ref, rows, h0))]
            for g, h0 in _KV_HEAD0.items()}
    _attend(q_ref, segs, o_ref, sink_ref, lam_init_ref, bl_ref, subln_ref, s_ref[0])


def _attn_param_specs(n_grid):
    def layer_map(*args):
        return (args[n_grid][0], 0, 0)

    smem = pl.BlockSpec(memory_space=pltpu.SMEM)
    return [smem, smem,
            pl.BlockSpec((None, 4, B_QK_DIM), layer_map),
            pl.BlockSpec((None, 1, HEAD_DIM), layer_map)]


_K_COL_BLOCK = Q_WIDTH // KV_WIDTH
_V_COL_BLOCK = _K_COL_BLOCK + 1


def _ctx_attention(sel, qkv, a_sink, lam_init, b_lambda, b_subln, *, batch, seq):
    return pl.pallas_call(
        _ctx_attn_kernel,
        out_shape=jax.ShapeDtypeStruct((qkv.shape[0], Q_WIDTH), BF16),
        grid_spec=pltpu.PrefetchScalarGridSpec(
            num_scalar_prefetch=1,
            grid=(batch,),
            in_specs=[pl.BlockSpec((seq, Q_WIDTH), lambda b, s: (b, 0)),
                      pl.BlockSpec((seq, KV_WIDTH), lambda b, s: (b, _K_COL_BLOCK)),
                      pl.BlockSpec((seq, KV_WIDTH), lambda b, s: (b, _V_COL_BLOCK))] + _attn_param_specs(1),
            out_specs=pl.BlockSpec((seq, Q_WIDTH), lambda b, s: (b, 0)),
            scratch_shapes=[pltpu.VMEM((seq, KV_HEADS * VX), BF16)]),
        compiler_params=_params(("arbitrary",), 40 * 1024 * 1024),
        name="context_attention",
    )(sel, qkv, qkv, qkv, a_sink, lam_init, b_lambda, b_subln)


def _lat_attn_kernel(s_ref, o_in_ref, q_ref, k_ref, v_ref, cak_ref, cav_ref, cbk_ref, cbv_ref, cck_ref, ccv_ref,
                     sink_ref, lam_init_ref, bl_ref, subln_ref, o_ref, ck_ref, cvx_ref, vx_ref, *, lat_seq):
    del o_in_ref
    @pl.when(pl.program_id(1) == 0)
    def _():
        past = ck_ref.shape[0]
        h0 = 0
        for kref, vref in ((cak_ref, cav_ref), (cbk_ref, cbv_ref), (cck_ref, ccv_ref)):
            n_heads = kref.shape[0] // past
            for h in range(n_heads):
                head_rows = pl.ds(h, past, stride=n_heads)
                ck_ref[:, (h0 + h) * HEAD_DIM:(h0 + h + 1) * HEAD_DIM] = kref[head_rows, :].astype(BF16)
                _store_values_with_ones(vref[head_rows, :].astype(BF16), cvx_ref, 1, h0 + h)
            h0 += n_heads
        _store_values_with_ones(v_ref[...], vx_ref, KV_HEADS)

    tq = q_ref.shape[0]
    band = tq + 2 * WINDOW
    q0 = pl.program_id(1) * tq
    k0 = pl.multiple_of(jnp.clip(q0 - WINDOW, 0, lat_seq - band), WINDOW)
    qpos = q0 + lax.broadcasted_iota(jnp.int32, (tq, 1), 0)
    kpos = k0 + lax.broadcasted_iota(jnp.int32, (1, band), 1)
    local_mask = jnp.abs(qpos - kpos) <= WINDOW

    every = slice(None)
    lat_rows = {"a": pl.ds(k0, band), "b": every, "c": every}
    segs = {g: [(_key_loader(ck_ref, every, h0), _value_loader(cvx_ref, every, h0)),
                (_key_loader(k_ref, lat_rows[g], h0), _value_loader(vx_ref, lat_rows[g], h0))]
            for g, h0 in _KV_HEAD0.items()}
    _attend(q_ref, segs, o_ref, sink_ref, lam_init_ref, bl_ref, subln_ref, s_ref[0], local_mask=local_mask)


def _lat_attention(sel, o, qkv, caches, a_sink, lam_init, b_lambda, b_subln, *, n_ctx_rows, lat_batch, lat_seq):
    q_blk0 = n_ctx_rows // TQ_LAT
    kv_blk0 = n_ctx_rows // lat_seq
    n_q = lat_seq // TQ_LAT
    past = caches[0].shape[2] // _CACHE_HEADS[0]

    def cache_spec(a):
        return pl.BlockSpec((None, None) + a.shape[2:], lambda b, i, s: (b, s[0], 0, 0))

    return pl.pallas_call(
        functools.partial(_lat_attn_kernel, lat_seq=lat_seq),
        out_shape=jax.ShapeDtypeStruct(o.shape, o.dtype),
        grid_spec=pltpu.PrefetchScalarGridSpec(
            num_scalar_prefetch=1,
            grid=(lat_batch, n_q),
            in_specs=[pl.BlockSpec(memory_space=pl.ANY),
                      pl.BlockSpec((TQ_LAT, Q_WIDTH), lambda b, i, s: (q_blk0 + b * n_q + i, 0)),
                      pl.BlockSpec((lat_seq, KV_WIDTH), lambda b, i, s: (kv_blk0 + b, _K_COL_BLOCK)),
                      pl.BlockSpec((lat_seq, KV_WIDTH), lambda b, i, s: (kv_blk0 + b, _V_COL_BLOCK))]
                     + [cache_spec(a) for a in caches] + _attn_param_specs(2),
            out_specs=pl.BlockSpec((TQ_LAT, Q_WIDTH), lambda b, i, s: (q_blk0 + b * n_q + i, 0)),
            scratch_shapes=[pltpu.VMEM((past, KV_WIDTH), BF16),
                            pltpu.VMEM((past, KV_HEADS * VX), BF16),
                            pltpu.VMEM((lat_seq, KV_HEADS * VX), BF16)]),
        input_output_aliases={1: 0},
        compiler_params=_params(("arbitrary", "arbitrary"), V7X_VMEM_LIMIT_BYTES),
        name="latent_attention",
    )(sel, o, qkv, qkv, qkv, *caches, a_sink, lam_init, b_lambda, b_subln)


def _out_proj_kernel(s_ref, a_ref, x_ref, par_ref, w_ref, o_ref, rs_ref):
    n = pl.program_id(1)
    y = _dot(a_ref[...], w_ref[...])
    for slab in range(o_ref.shape[1] // TN_OUT):
        @pl.when(n == slab)
        def _():
            o_ref[:, slab * TN_OUT:(slab + 1) * TN_OUT] = y

    @pl.when(n == pl.num_programs(1) - 1)
    def _():
        _gated_residual_rows(x_ref, o_ref, o_ref, par_ref, rs_ref, 1.0)


def _out_proj(sel, a, x, params, w_out, *, n_ctx_rows, lat_rows):
    m, d = x.shape
    kdim = a.shape[1]
    cond = functools.partial(_cond_of_tile, tm=TM, n_ctx_rows=n_ctx_rows, lat_rows=lat_rows)
    return pl.pallas_call(
        _out_proj_kernel,
        out_shape=jax.ShapeDtypeStruct((m, d), F32),
        grid_spec=pltpu.PrefetchScalarGridSpec(
            num_scalar_prefetch=1,
            grid=(m // TM, d // TN_OUT),
            in_specs=[
                pl.BlockSpec((TM, kdim), lambda i, n, s: (i, 0)),
                pl.BlockSpec((TM, d), lambda i, n, s: (i, 0)),
                _sublayer_spec(d, cond, 2),
                pl.BlockSpec((None, kdim, TN_OUT), lambda i, n, s: (s[0], 0, n)),
            ],
            out_specs=pl.BlockSpec((TM, d), lambda i, n, s: (i, 0)),
            scratch_shapes=[pltpu.VMEM((TM, LANES), F32)]),
        compiler_params=_params(("arbitrary", "arbitrary"), V7X_VMEM_LIMIT_BYTES),
        name="out_projection",
    )(sel, a, x, params, w_out)


def _rope_tables(n_ctx_rows, lat_batch, lat_seq):
    t = np.arange(lat_seq)
    row = (t // GRID_W).astype(np.float32)
    col = (t % GRID_W).astype(np.float32)

    def tables(dim):
        nf = dim // 4
        inv = jnp.asarray(ROPE_THETA, F32) ** (-jnp.arange(nf, dtype=F32) / nf)
        ang = jnp.concatenate([row[:, None] * inv, col[:, None] * inv], axis=-1)
        cos, sin = jnp.cos(ang), jnp.sin(ang)
        cos_t = jnp.tile(jnp.concatenate([cos, cos], axis=-1), (1, LANES // dim))
        sin_t = jnp.tile(jnp.concatenate([-sin, sin], axis=-1), (1, LANES // dim))
        return (jnp.concatenate([jnp.ones((n_ctx_rows, LANES), F32)] + [cos_t] * lat_batch, axis=0),
                jnp.concatenate([jnp.zeros((n_ctx_rows, LANES), F32)] + [sin_t] * lat_batch, axis=0))

    return jnp.concatenate(tables(HEAD_DIM) + tables(B_QK_DIM), axis=1)


def kernel(x_prompt, x_sample, cache_a_k, cache_a_v, cache_b_k, cache_b_v, cache_c_k, cache_c_v,
           c, c_ctx, w_mod, b_mod, norm_pre, norm_post, ffn_gate, ffn_up, ffn_down,
           w_in, w_out, a_sink, b_lambda, b_subln, c_qnorm, c_knorm):
    batch, seq, d = x_prompt.shape
    lat_batch, lat_seq, _ = x_sample.shape
    depth = w_mod.shape[0]
    past = cache_a_k.shape[2]
    n_ctx_rows = batch * seq
    assert d == D_MODEL and depth == DEPTH and w_in.shape[-1] == IN_WIDTH
    assert n_ctx_rows % TM == 0 and lat_seq % TM == 0 and 1 + lat_batch <= N_COND
    assert n_ctx_rows % lat_seq == 0 and lat_seq % TQ_LAT == 0 and TM % seq == 0

    x = jnp.concatenate([x_prompt.reshape(n_ctx_rows, d), x_sample.reshape(lat_batch * lat_seq, d)], axis=0)
    cond = jnp.concatenate([c_ctx[None, :], c, jnp.zeros((N_COND - 1 - lat_batch, d), F32)], axis=0)
    mod = _modulation(cond, w_mod, b_mod).reshape(depth, N_COND, N_SUB, 3, d)
    params = _sublayer_params(mod, norm_pre, norm_post)
    rope = _rope_tables(n_ctx_rows, lat_batch, lat_seq)
    w_in = w_in.astype(BF16)
    w_out = w_out.astype(BF16)
    caches = [a.reshape(lat_batch, depth, -1, HEAD_DIM) for a in
              (cache_a_k, cache_a_v, cache_b_k, cache_b_v, cache_c_k, cache_c_v)]
    lam_init = jnp.asarray([0.8 - 0.6 * math.exp(-0.3 * l) for l in range(depth)], F32)
    subln = b_subln.reshape(depth, 1, HEAD_DIM)
    qk_gains = jnp.stack([c_qnorm, c_knorm], axis=1)
    geom = dict(n_ctx_rows=n_ctx_rows, lat_rows=lat_seq)

    new_caches = None
    for l in range(depth):
        sel = lambda sub, which=0: jnp.asarray([l, sub, which], jnp.int32)
        x = _ffn(sel(0, 0), x, params, ffn_gate, ffn_up, ffn_down, **geom)
        qkv, *new_caches = _in_proj(sel(1), x, params, w_in, rope, qk_gains, new_caches,
                                    batch=batch, seq=seq, **geom)
        o = _ctx_attention(sel(1), qkv, a_sink, lam_init, b_lambda, subln, batch=batch, seq=seq)
        o = _lat_attention(sel(1), o, qkv, caches, a_sink, lam_init, b_lambda, subln,
                           n_ctx_rows=n_ctx_rows, lat_batch=lat_batch, lat_seq=lat_seq)
        x = _out_proj(sel(1), o, x, params, w_out, **geom)
        x = _ffn(sel(2, 1), x, params, ffn_gate, ffn_up, ffn_down, **geom)

    new_kv = [a.reshape(batch, depth, seq, heads, HEAD_DIM) for a, heads in zip(new_caches, _CACHE_HEADS)]

    y_prompt = x[:n_ctx_rows].reshape(batch, seq, d)
    y_sample = x[n_ctx_rows:].reshape(lat_batch, lat_seq, d)
    return (y_prompt, y_sample) + tuple(new_kv)
```
